```python
import math
import jax, jax.numpy as jnp
from jax import lax
import numpy as np

D_MODEL = 1024
BATCH = 4
SEQ = 4096
DEPTH = 4
DEC_BATCH = 128
DEC_SEQ = 4
PAST_LEN = 8192
PAGE_SIZE = 128

N_MIXERS = 2
N_ATTN_LAYERS = (DEPTH + 1) // 2
N_CONV_LAYERS = DEPTH // 2
HEAD_DIM = 64
N_HEADS = D_MODEL // HEAD_DIM
N_KV_HEADS = 4
GROUP = N_HEADS // N_KV_HEADS
QKV_DIM = (N_HEADS + 2 * N_KV_HEADS) * HEAD_DIM
WINDOW = 128
BLOCK = 128
ATTN_SCALE = HEAD_DIM ** -0.5
NEG = -1e30
NUM_BUCKETS = 32
MAX_DISTANCE = 128
D_CONV = D_MODEL
CONV_W = 3
N_KEYS = 128
N_EXPERTS = N_KEYS * N_KEYS
PEER_HEADS = 8
D_KEY = 256
D_KEY_HALF = D_KEY // 2
PEER_TOPK = 16
PEER_CHUNK = 256
EPS = 1e-6

kernel_name = "hybrid_swa_shortconv_peer_step"


def rmsnorm(x, g):
    xf = x.astype(jnp.float32)
    y = xf * lax.rsqrt(jnp.mean(xf * xf, axis=-1, keepdims=True) + EPS)
    return (y * g.astype(jnp.float32)).astype(x.dtype)


def t5_bucket(dist):
    n = jnp.maximum(dist, 0)
    max_exact = NUM_BUCKETS // 2
    nf = jnp.maximum(n, 1).astype(jnp.float32)
    large = max_exact + (jnp.log(nf / max_exact) / math.log(MAX_DISTANCE / max_exact)
                         * (NUM_BUCKETS - max_exact)).astype(jnp.int32)
    large = jnp.minimum(large, NUM_BUCKETS - 1)
    return jnp.where(n < max_exact, n, large)


def rel_pos_bias(dist, rel_bias):
    return jnp.transpose(jnp.take(rel_bias, t5_bucket(dist), axis=0), (2, 0, 1))


def qkv_split(xn, w_qkv):
    B, T, _ = xn.shape
    qkv = xn @ w_qkv
    hq = N_HEADS * HEAD_DIM
    hk = N_KV_HEADS * HEAD_DIM
    q = qkv[..., :hq].reshape(B, T, N_HEADS, HEAD_DIM)
    k = qkv[..., hq:hq + hk].reshape(B, T, N_KV_HEADS, HEAD_DIM)
    v = qkv[..., hq + hk:].reshape(B, T, N_KV_HEADS, HEAD_DIM)
    return q, k, v


def window_attention(q, k, v, bias, mask, sinks):
    B, N, Q, _, _ = q.shape
    C = k.shape[2]
    qg = q.reshape(B, N, Q, N_KV_HEADS, GROUP, HEAD_DIM)
    s = jnp.einsum('bnqhgd,bnchd->bnhgqc', qg, k, preferred_element_type=jnp.float32) * ATTN_SCALE
    s = s + bias.reshape(N_KV_HEADS, GROUP, Q, C).astype(jnp.float32)
    s = jnp.where(mask[None, :, None, None], s, NEG)
    sink = sinks.reshape(N_KV_HEADS, GROUP).astype(jnp.float32)[:, :, None, None]
    m = jnp.maximum(jnp.max(s, axis=-1, keepdims=True), sink)
    p = jnp.exp(s - m)
    p = p / (jnp.sum(p, axis=-1, keepdims=True) + jnp.exp(sink - m))
    o = jnp.einsum('bnhgqc,bnchd->bnqhgd', p, v.astype(jnp.float32))
    return o.reshape(B, N, Q, N_HEADS * HEAD_DIM).astype(q.dtype)


def swa_prompt(xn, w_qkv, sinks, w_o, rel_bias):
    B, S, _ = xn.shape
    nb = S // BLOCK
    q, k, v = qkv_split(xn, w_qkv)
    pad = jnp.zeros((B, BLOCK, N_KV_HEADS, HEAD_DIM), k.dtype)
    def band(t):
        prev = jnp.concatenate([pad, t], axis=1)[:, :S].reshape(B, nb, BLOCK, N_KV_HEADS, HEAD_DIM)
        return jnp.concatenate([prev, t.reshape(B, nb, BLOCK, N_KV_HEADS, HEAD_DIM)], axis=2)
    kb, vb = band(k), band(v)
    r = jnp.arange(BLOCK, dtype=jnp.int32)
    c = jnp.arange(2 * BLOCK, dtype=jnp.int32)
    dist = r[:, None] - c[None, :] + BLOCK
    kpos = (jnp.arange(nb, dtype=jnp.int32)[:, None] - 1) * BLOCK + c[None, :]
    mask = ((dist >= 0) & (dist < WINDOW))[None] & (kpos >= 0)[:, None, :]
    o = window_attention(q.reshape(B, nb, BLOCK, N_HEADS, HEAD_DIM), kb, vb,
                         rel_pos_bias(dist, rel_bias), mask, sinks)
    out = o.reshape(B, S, N_HEADS * HEAD_DIM) @ w_o
    rows = min(WINDOW, S)
    return out, k[:, S - rows:], v[:, S - rows:]


def swa_sample(xn, cache_k, cache_v, w_qkv, sinks, w_o, rel_bias):
    B, T, _ = xn.shape
    W = cache_k.shape[1]
    q, k, v = qkv_split(xn, w_qkv)
    kc = jnp.concatenate([cache_k.astype(k.dtype), k], axis=1)
    vc = jnp.concatenate([cache_v.astype(v.dtype), v], axis=1)
    j = jnp.arange(T, dtype=jnp.int32)
    c = jnp.arange(W + T, dtype=jnp.int32)
    dist = j[:, None] + W - c[None, :]
    mask = ((dist >= 0) & (dist < WINDOW))[None]
    o = window_attention(q[:, None], kc[:, None], vc[:, None], rel_pos_bias(dist, rel_bias), mask, sinks)
    out = o.reshape(B, T, N_HEADS * HEAD_DIM) @ w_o
    return out, kc[:, -W:], vc[:, -W:]


def short_conv(xn, buf, w_in, w_conv, w_out):
    T = xn.shape[1]
    bch = xn @ w_in
    b_gate = bch[..., :D_CONV]
    c_gate = bch[..., D_CONV:2 * D_CONV]
    h = bch[..., 2 * D_CONV:]
    u = c_gate * h
    up = jnp.concatenate([buf.astype(u.dtype), u], axis=1)
    y = sum(w_conv[i] * up[:, i:i + T] for i in range(CONV_W))
    return (b_gate * y) @ w_out, up[:, -(CONV_W - 1):]


def peer(xn, w_q, sub_keys, u_tab, v_tab):
    shp = xn.shape
    x2 = xn.reshape(-1, D_MODEL)
    T = x2.shape[0]
    n_chunks = -(-T // PEER_CHUNK)
    xp = jnp.pad(x2, ((0, n_chunks * PEER_CHUNK - T), (0, 0))).reshape(n_chunks, PEER_CHUNK, D_MODEL)

    def one(xc):
        q = (xc @ w_q).reshape(PEER_CHUNK, PEER_HEADS, 2, D_KEY_HALF)
        s = jnp.einsum('thpd,hpkd->thpk', q, sub_keys)
        sv, si = lax.top_k(s, PEER_TOPK)
        cand = (sv[:, :, 0, :, None] + sv[:, :, 1, None, :]).reshape(PEER_CHUNK, PEER_HEADS, -1)
        cidx = (si[:, :, 0, :, None] * N_KEYS + si[:, :, 1, None, :]).reshape(PEER_CHUNK, PEER_HEADS, -1)
        best, pos = lax.top_k(cand, PEER_TOPK)
        idx = jnp.take_along_axis(cidx, pos, axis=-1)
        g = jax.nn.softmax(best.astype(jnp.float32), axis=-1).astype(xc.dtype)
        u = jnp.take(u_tab, idx, axis=0)
        act = jax.nn.gelu(jnp.einsum('td,thkd->thk', xc, u), approximate=False)
        v = jnp.take(v_tab, idx, axis=0)
        return jnp.einsum('thk,thkd->td', g * act, v)

    y = lax.map(one, xp).reshape(-1, D_MODEL)[:T]
    return y.reshape(shp)


def setup_inputs(seed: int = 0) -> dict:
    key = jax.random.key(seed)
    ks = jax.random.split(key, 20)
    n = jax.random.normal
    f32 = jnp.float32
    win_rows = min(WINDOW, PAST_LEN)
    return {
        "x_prompt": n(ks[0], (BATCH, SEQ, D_MODEL), f32),
        "x_sample": n(ks[1], (DEC_BATCH, DEC_SEQ, D_MODEL), f32),
        "cache_k": n(ks[2], (N_ATTN_LAYERS, DEC_BATCH, win_rows, N_KV_HEADS, HEAD_DIM), f32),
        "cache_v": n(ks[3], (N_ATTN_LAYERS, DEC_BATCH, win_rows, N_KV_HEADS, HEAD_DIM), f32),
        "state_conv": n(ks[4], (N_CONV_LAYERS, DEC_BATCH, CONV_W - 1, D_CONV), f32),
        "norm_mix_g": 1.0 + 0.02 * n(ks[5], (DEPTH, D_MODEL), f32),
        "norm_ffn_g": 1.0 + 0.02 * n(ks[6], (DEPTH, D_MODEL), f32),
        "norm_final_g": 1.0 + 0.02 * n(ks[7], (D_MODEL,), f32),
        "rel_bias": 0.5 * n(ks[8], (NUM_BUCKETS, N_HEADS), f32),
        "attn_w_qkv": n(ks[9], (N_ATTN_LAYERS, D_MODEL, QKV_DIM), f32) * D_MODEL ** -0.5,
        "attn_sinks": n(ks[10], (N_ATTN_LAYERS, N_HEADS), f32),
        "attn_w_o": n(ks[11], (N_ATTN_LAYERS, N_HEADS * HEAD_DIM, D_MODEL), f32) * (N_HEADS * HEAD_DIM) ** -0.5,
        "conv_w_in": n(ks[12], (N_CONV_LAYERS, D_MODEL, 3 * D_CONV), f32) * D_MODEL ** -0.5,
        "conv_w": n(ks[13], (N_CONV_LAYERS, CONV_W, D_CONV), f32) * CONV_W ** -0.5,
        "conv_w_out": n(ks[14], (N_CONV_LAYERS, D_CONV, D_MODEL), f32) * D_CONV ** -0.5,
        "peer_w_q": n(ks[15], (DEPTH, D_MODEL, PEER_HEADS * D_KEY), f32) * D_MODEL ** -0.5,
        "peer_sub_keys": n(ks[16], (DEPTH, PEER_HEADS, 2, N_KEYS, D_KEY_HALF), f32) * D_KEY_HALF ** -0.5,
        "peer_u": n(ks[17], (DEPTH, N_EXPERTS, D_MODEL), f32) * D_MODEL ** -0.5,
        "peer_v": n(ks[18], (DEPTH, N_EXPERTS, D_MODEL), f32) * (PEER_HEADS * PEER_TOPK) ** -0.5,
    }


def reference(x_prompt, x_sample, cache_k, cache_v, state_conv, norm_mix_g, norm_ffn_g, norm_final_g,
              rel_bias, attn_w_qkv, attn_sinks, attn_w_o, conv_w_in, conv_w, conv_w_out,
              peer_w_q, peer_sub_keys, peer_u, peer_v):
    hp, hs = x_prompt, x_sample
    kp_l, vp_l, cp_l, ks_l, vs_l, cs_l = [], [], [], [], [], []
    for i in range(DEPTH):
        j = i // N_MIXERS
        g = norm_mix_g[i]
        if i % N_MIXERS == 0:
            mp, kp, vp = swa_prompt(rmsnorm(hp, g), attn_w_qkv[j], attn_sinks[j], attn_w_o[j], rel_bias)
            ms, kn, vn = swa_sample(rmsnorm(hs, g), cache_k[j], cache_v[j], attn_w_qkv[j],
                                    attn_sinks[j], attn_w_o[j], rel_bias)
            kp_l.append(kp); vp_l.append(vp); ks_l.append(kn); vs_l.append(vn)
        else:
            zero_buf = jnp.zeros((hp.shape[0], CONV_W - 1, D_CONV), hp.dtype)
            mp, cp = short_conv(rmsnorm(hp, g), zero_buf, conv_w_in[j], conv_w[j], conv_w_out[j])
            ms, cn = short_conv(rmsnorm(hs, g), state_conv[j], conv_w_in[j], conv_w[j], conv_w_out[j])
            cp_l.append(cp); cs_l.append(cn)
        hp = hp + mp
        hs = hs + ms
        gf = norm_ffn_g[i]
        hp = hp + peer(rmsnorm(hp, gf), peer_w_q[i], peer_sub_keys[i], peer_u[i], peer_v[i])
        hs = hs + peer(rmsnorm(hs, gf), peer_w_q[i], peer_sub_keys[i], peer_u[i], peer_v[i])
    y_prompt = rmsnorm(hp, norm_final_g)
    y_sample = rmsnorm(hs, norm_final_g)
    return (y_prompt, y_sample,
            jnp.stack(kp_l), jnp.stack(vp_l), jnp.stack(cp_l),
            jnp.stack(ks_l), jnp.stack(vs_l), jnp.stack(cs_l))
```

```python
import functools
import math

import jax
import jax.numpy as jnp
import numpy as np
from jax import lax
from jax.experimental import pallas as pl
from jax.experimental.pallas import tpu as pltpu

F32 = jnp.float32
BF16 = jnp.bfloat16

D_MODEL = 1024
BATCH = 4
SEQ = 4096
DEPTH = 4
DEC_BATCH = 128
DEC_SEQ = 4
HEAD_DIM = 64
N_HEADS = 16
N_KV_HEADS = 4
GROUP = N_HEADS // N_KV_HEADS
QKV_DIM = (N_HEADS + 2 * N_KV_HEADS) * HEAD_DIM
KV_DIM = N_KV_HEADS * HEAD_DIM
WINDOW = 128
BLOCK = 128
ATTN_SCALE = HEAD_DIM ** -0.5
NEG = -1e30
NUM_BUCKETS = 32
MAX_DISTANCE = 128
D_CONV = D_MODEL
CONV_W = 3
N_KEYS = 128
N_EXPERTS = N_KEYS * N_KEYS
PEER_HEADS = 8
D_KEY_HALF = 128
PEER_TOPK = 16
EPS = 1e-6

N_PROMPT = BATCH * SEQ
N_SAMPLE = DEC_BATCH * DEC_SEQ
N_TOK = N_PROMPT + N_SAMPLE

LANES = 128
SUBLANES = 8
VMEM_LIMIT_BYTES = 56 * 1024 * 1024

ROW_TILE = 512
N_ROW_TILES = N_TOK // ROW_TILE
PROMPT_ROW_TILES = N_PROMPT // ROW_TILE
TILES_PER_SEQ = SEQ // ROW_TILE
ROUTE_TILE = 256
DENSE_TILE = 512
EXPERT_BLOCK = 1024
KEYS_PER_BLOCK = EXPERT_BLOCK // N_KEYS
SAMPLE_SEQ_BLOCK = 8
KEYS_SAMPLE = WINDOW + DEC_SEQ
KEYS_SAMPLE_PAD = 2 * LANES
CAND_MAIN_ROWS = 8
CAND_ROWS = CAND_MAIN_ROWS * PEER_TOPK + (PEER_TOPK - CAND_MAIN_ROWS)
NOT_SELECTED_RANK = 127.0


def _params(semantics):
    return pltpu.CompilerParams(dimension_semantics=semantics, vmem_limit_bytes=VMEM_LIMIT_BYTES)


def _rmsnorm(x, g):
    ms = jnp.mean(x * x, axis=-1, keepdims=True)
    return x * lax.rsqrt(ms + EPS) * g


def _norm_matmul_kernel(x_ref, g_ref, w_ref, o_ref):
    xn = _rmsnorm(x_ref[...], g_ref[...]).astype(BF16)
    o_ref[...] = jnp.dot(xn, w_ref[...], preferred_element_type=F32)


def _norm_matmul(x, g, w_bf16):
    n_out = w_bf16.shape[1]
    return pl.pallas_call(
        _norm_matmul_kernel,
        out_shape=jax.ShapeDtypeStruct((N_TOK, n_out), F32),
        grid=(N_ROW_TILES,),
        in_specs=[
            pl.BlockSpec((ROW_TILE, D_MODEL), lambda i: (i, 0)),
            pl.BlockSpec((1, D_MODEL), lambda i: (0, 0)),
            pl.BlockSpec((D_MODEL, n_out), lambda i: (0, 0)),
        ],
        out_specs=pl.BlockSpec((ROW_TILE, n_out), lambda i: (i, 0)),
        compiler_params=_params(("parallel",)),
        name="norm_matmul",
    )(x, g.reshape(1, D_MODEL), w_bf16)


def _matmul_residual_kernel(a_ref, w_ref, r_ref, o_ref):
    o_ref[...] = r_ref[...] + jnp.dot(a_ref[...], w_ref[...], preferred_element_type=F32)


def _matmul_residual(a_bf16, w_bf16, res):
    return pl.pallas_call(
        _matmul_residual_kernel,
        out_shape=jax.ShapeDtypeStruct((N_TOK, D_MODEL), F32),
        grid=(N_ROW_TILES,),
        in_specs=[
            pl.BlockSpec((ROW_TILE, D_MODEL), lambda i: (i, 0)),
            pl.BlockSpec((D_MODEL, D_MODEL), lambda i: (0, 0)),
            pl.BlockSpec((ROW_TILE, D_MODEL), lambda i: (i, 0)),
        ],
        out_specs=pl.BlockSpec((ROW_TILE, D_MODEL), lambda i: (i, 0)),
        compiler_params=_params(("parallel",)),
        name="matmul_residual",
    )(a_bf16, w_bf16, res)


def _final_norm_kernel(x_ref, g_ref, o_ref):
    o_ref[...] = _rmsnorm(x_ref[...], g_ref[...])


def _final_norm(x, g):
    return pl.pallas_call(
        _final_norm_kernel,
        out_shape=jax.ShapeDtypeStruct((N_TOK, D_MODEL), F32),
        grid=(N_ROW_TILES,),
        in_specs=[
            pl.BlockSpec((ROW_TILE, D_MODEL), lambda i: (i, 0)),
            pl.BlockSpec((1, D_MODEL), lambda i: (0, 0)),
        ],
        out_specs=pl.BlockSpec((ROW_TILE, D_MODEL), lambda i: (i, 0)),
        compiler_params=_params(("parallel",)),
        name="final_norm",
    )(x, g.reshape(1, D_MODEL))


def _t5_bucket_np(dist):
    n = np.maximum(dist, 0)
    max_exact = NUM_BUCKETS // 2
    nf = np.maximum(n, 1).astype(np.float32)
    large = max_exact + (np.log(nf / np.float32(max_exact)) / np.float32(math.log(MAX_DISTANCE / max_exact))
                         * np.float32(NUM_BUCKETS - max_exact)).astype(np.int32)
    large = np.minimum(large, NUM_BUCKETS - 1)
    return np.where(n < max_exact, n, large).astype(np.int32)


def _bias_table_kernel(rel_ref, bucket_ref, o_ref):
    bucket = bucket_ref[...]
    for h in range(N_HEADS):
        acc = jnp.full(bucket.shape, NEG, F32)
        for b in range(NUM_BUCKETS):
            acc = jnp.where(bucket == b, rel_ref[b, h], acc)
        o_ref[h] = acc


def _bias_table(rel_bias, bucket_np):
    rows, cols = bucket_np.shape
    return pl.pallas_call(
        _bias_table_kernel,
        out_shape=jax.ShapeDtypeStruct((N_HEADS, rows, cols), F32),
        in_specs=[
            pl.BlockSpec(memory_space=pltpu.SMEM),
            pl.BlockSpec((rows, cols), lambda: (0, 0)),
        ],
        out_specs=pl.BlockSpec((N_HEADS, rows, cols), lambda: (0, 0, 0)),
        name="bias_table",
    )(rel_bias, jnp.asarray(bucket_np))


def _prompt_buckets():
    r = np.arange(BLOCK)[:, None]
    c = np.arange(2 * BLOCK)[None, :]
    dist = r - c + BLOCK
    valid = (dist >= 0) & (dist < WINDOW)
    rest = np.where(valid, _t5_bucket_np(dist), -1).astype(np.int32)
    first = np.where(c >= BLOCK, rest, -1).astype(np.int32)
    return np.concatenate([first, rest], axis=0)


def _sample_buckets():
    j = np.arange(DEC_SEQ)[:, None]
    c = np.arange(KEYS_SAMPLE_PAD)[None, :]
    dist = j + WINDOW - c
    valid = (dist >= 0) & (dist < WINDOW) & (c < KEYS_SAMPLE)
    return np.where(valid, _t5_bucket_np(dist), -1).astype(np.int32)


def _softmax_sink_pv(s_list, v_list, sink):
    m = sink
    for s in s_list:
        m = jnp.maximum(m, jnp.max(s, axis=-1, keepdims=True))
    denom = jnp.exp(sink - m)
    acc = None
    for s, v in zip(s_list, v_list):
        p = jnp.exp(s - m)
        denom = denom + jnp.sum(p, axis=-1, keepdims=True)
        pv = jnp.dot(p.astype(BF16), v, preferred_element_type=F32)
        acc = pv if acc is None else acc + pv
    return acc / denom


def _attn_prompt_kernel(sink_ref, q_ref, kp_ref, kc_ref, vp_ref, vc_ref, bias_ref, o_ref):
    kp = kp_ref[...].astype(BF16)
    kc = kc_ref[...].astype(BF16)
    vp = vp_ref[...].astype(BF16)
    vc = vc_ref[...].astype(BF16)
    outs = []
    for g in range(N_KV_HEADS):
        ksl = slice(g * HEAD_DIM, (g + 1) * HEAD_DIM)
        qg = jnp.concatenate(
            [q_ref[:, (g * GROUP + hh) * HEAD_DIM:(g * GROUP + hh + 1) * HEAD_DIM] for hh in range(GROUP)],
            axis=0).astype(BF16)
        bias = jnp.concatenate([bias_ref[g * GROUP + hh] for hh in range(GROUP)], axis=0)
        sink = jnp.concatenate(
            [jnp.full((BLOCK, 1), sink_ref[g * GROUP + hh], F32) for hh in range(GROUP)], axis=0)
        dn = (((1,), (1,)), ((), ()))
        s_prev = lax.dot_general(qg, kp[:, ksl], dn, preferred_element_type=F32) * ATTN_SCALE + bias[:, :BLOCK]
        s_cur = lax.dot_general(qg, kc[:, ksl], dn, preferred_element_type=F32) * ATTN_SCALE + bias[:, BLOCK:]
        og = _softmax_sink_pv([s_prev, s_cur], [vp[:, ksl], vc[:, ksl]], sink)
        for hh in range(GROUP):
            outs.append(og[hh * BLOCK:(hh + 1) * BLOCK])
    o_ref[...] = jnp.concatenate(outs, axis=1).astype(o_ref.dtype)


def _attn_prompt(qkv, sinks, bias_tab):
    nb = SEQ // BLOCK
    kcol = N_HEADS * HEAD_DIM // KV_DIM
    row = lambda b, n: b * nb + n
    prev = lambda b, n: b * nb + jnp.maximum(n - 1, 0)
    return pl.pallas_call(
        _attn_prompt_kernel,
        out_shape=jax.ShapeDtypeStruct((N_PROMPT, D_MODEL), BF16),
        grid=(BATCH, nb),
        in_specs=[
            pl.BlockSpec(memory_space=pltpu.SMEM),
            pl.BlockSpec((BLOCK, N_HEADS * HEAD_DIM), lambda b, n: (row(b, n), 0)),
            pl.BlockSpec((BLOCK, KV_DIM), lambda b, n: (prev(b, n), kcol)),
            pl.BlockSpec((BLOCK, KV_DIM), lambda b, n: (row(b, n), kcol)),
            pl.BlockSpec((BLOCK, KV_DIM), lambda b, n: (prev(b, n), kcol + 1)),
            pl.BlockSpec((BLOCK, KV_DIM), lambda b, n: (row(b, n), kcol + 1)),
            pl.BlockSpec((N_HEADS, BLOCK, 2 * BLOCK), lambda b, n: (0, jnp.minimum(n, 1), 0)),
        ],
        out_specs=pl.BlockSpec((BLOCK, D_MODEL), lambda b, n: (row(b, n), 0)),
        compiler_params=_params(("parallel", "parallel")),
        name="attn_prompt",
    )(sinks, qkv, qkv, qkv, qkv, qkv, bias_tab)


def _attn_sample_kernel(q_ref, k_ref, v_ref, bias_ref, sink_ref, o_ref):
    nb = SAMPLE_SEQ_BLOCK * N_KV_HEADS
    rows = DEC_SEQ * GROUP
    q = q_ref[...].reshape(nb, rows, HEAD_DIM)
    k = k_ref[...].reshape(nb, KEYS_SAMPLE_PAD, HEAD_DIM)
    v = v_ref[...].reshape(nb, KEYS_SAMPLE_PAD, HEAD_DIM)
    s = jnp.einsum("nqd,nkd->nqk", q, k, preferred_element_type=F32) * ATTN_SCALE
    s = s.reshape(SAMPLE_SEQ_BLOCK, N_KV_HEADS, rows, KEYS_SAMPLE_PAD) + bias_ref[...][None]
    sink = sink_ref[...][None]
    m = jnp.maximum(jnp.max(s, axis=-1, keepdims=True), sink)
    p = jnp.exp(s - m)
    denom = jnp.sum(p, axis=-1, keepdims=True) + jnp.exp(sink - m)
    pv = jnp.einsum("nqk,nkd->nqd", p.reshape(nb, rows, KEYS_SAMPLE_PAD).astype(BF16), v,
                    preferred_element_type=F32)
    o_ref[...] = pv.reshape(SAMPLE_SEQ_BLOCK, N_KV_HEADS, rows, HEAD_DIM) / denom


def _attn_sample(qg, kc, vc, bias_s, sink_s):
    rows = DEC_SEQ * GROUP
    blk = lambda shape: pl.BlockSpec((SAMPLE_SEQ_BLOCK,) + shape, lambda i: (i, 0, 0, 0))
    return pl.pallas_call(
        _attn_sample_kernel,
        out_shape=jax.ShapeDtypeStruct((DEC_BATCH, N_KV_HEADS, rows, HEAD_DIM), F32),
        grid=(DEC_BATCH // SAMPLE_SEQ_BLOCK,),
        in_specs=[
            blk((N_KV_HEADS, rows, HEAD_DIM)),
            blk((N_KV_HEADS, KEYS_SAMPLE_PAD, HEAD_DIM)),
            blk((N_KV_HEADS, KEYS_SAMPLE_PAD, HEAD_DIM)),
            pl.BlockSpec((N_KV_HEADS, rows, KEYS_SAMPLE_PAD), lambda i: (0, 0, 0)),
            pl.BlockSpec((N_KV_HEADS, rows, 1), lambda i: (0, 0, 0)),
        ],
        out_specs=blk((N_KV_HEADS, rows, HEAD_DIM)),
        compiler_params=_params(("parallel",)),
        name="attn_sample",
    )(qg, kc, vc, bias_s, sink_s)


def _conv_prompt_kernel(bch_ref, w_ref, wout_ref, r_ref, o_ref, ulast_ref, ubuf):
    i = pl.program_id(0)

    @pl.when(i % TILES_PER_SEQ == 0)
    def _():
        ubuf[0:SUBLANES, :] = jnp.zeros((SUBLANES, D_CONV), F32)

    u = bch_ref[:, D_CONV:2 * D_CONV] * bch_ref[:, 2 * D_CONV:]
    ubuf[SUBLANES:, :] = u
    y = (w_ref[0:1, :] * ubuf[SUBLANES - 2:SUBLANES - 2 + ROW_TILE, :]
         + w_ref[1:2, :] * ubuf[SUBLANES - 1:SUBLANES - 1 + ROW_TILE, :]
         + w_ref[2:3, :] * u)
    gated = (bch_ref[:, :D_CONV] * y).astype(BF16)
    o_ref[...] = r_ref[...] + jnp.dot(gated, wout_ref[...], preferred_element_type=F32)
    tail = u[ROW_TILE - SUBLANES:, :]
    ulast_ref[...] = tail
    ubuf[0:SUBLANES, :] = tail


def _conv_prompt(bch, conv_w, wout_bf16, res):
    return pl.pallas_call(
        _conv_prompt_kernel,
        out_shape=(jax.ShapeDtypeStruct((N_PROMPT, D_MODEL), F32),
                   jax.ShapeDtypeStruct((PROMPT_ROW_TILES * SUBLANES, D_CONV), F32)),
        grid=(PROMPT_ROW_TILES,),
        in_specs=[
            pl.BlockSpec((ROW_TILE, 3 * D_CONV), lambda i: (i, 0)),
            pl.BlockSpec((CONV_W, D_CONV), lambda i: (0, 0)),
            pl.BlockSpec((D_CONV, D_MODEL), lambda i: (0, 0)),
            pl.BlockSpec((ROW_TILE, D_MODEL), lambda i: (i, 0)),
        ],
        out_specs=(pl.BlockSpec((ROW_TILE, D_MODEL), lambda i: (i, 0)),
                   pl.BlockSpec((SUBLANES, D_CONV), lambda i: (i, 0))),
        scratch_shapes=[pltpu.VMEM((ROW_TILE + SUBLANES, D_CONV), F32)],
        compiler_params=_params(("arbitrary",)),
        name="conv_prompt",
    )(bch, conv_w, wout_bf16, res)


def _conv_sample_kernel(bch_ref, st_ref, w_ref, wout_ref, r_ref, o_ref, u_ref):
    u = bch_ref[:, D_CONV:2 * D_CONV] * bch_ref[:, 2 * D_CONV:]
    u_ref[...] = u
    up = [st_ref[0:DEC_BATCH, :], st_ref[DEC_BATCH:, :]] + [u[t * DEC_BATCH:(t + 1) * DEC_BATCH] for t in range(DEC_SEQ)]
    y = jnp.concatenate(
        [w_ref[0:1, :] * up[t] + w_ref[1:2, :] * up[t + 1] + w_ref[2:3, :] * up[t + 2] for t in range(DEC_SEQ)], axis=0)
    gated = (bch_ref[:, :D_CONV] * y).astype(BF16)
    o_ref[...] = r_ref[...] + jnp.dot(gated, wout_ref[...], preferred_element_type=F32)


def _conv_sample(bch_s, state_tm, conv_w, wout_bf16, res_s):
    full = lambda shape: pl.BlockSpec(shape, lambda: (0,) * len(shape))
    return pl.pallas_call(
        _conv_sample_kernel,
        out_shape=(jax.ShapeDtypeStruct((N_SAMPLE, D_MODEL), F32),
                   jax.ShapeDtypeStruct((N_SAMPLE, D_CONV), F32)),
        in_specs=[full((N_SAMPLE, 3 * D_CONV)), full(((CONV_W - 1) * DEC_BATCH, D_CONV)),
                  full((CONV_W, D_CONV)), full((D_CONV, D_MODEL)), full((N_SAMPLE, D_MODEL))],
        out_specs=(full((N_SAMPLE, D_MODEL)), full((N_SAMPLE, D_CONV))),
        compiler_params=pltpu.CompilerParams(vmem_limit_bytes=VMEM_LIMIT_BYTES),
        name="conv_sample",
    )(bch_s, state_tm, conv_w, wout_bf16, res_s)


def _extract_max(vals, idx):
    m = jnp.max(vals, axis=0, keepdims=True)
    first = jnp.min(jnp.where(vals == m, idx, jnp.float32(1e9)), axis=0, keepdims=True)
    hit = idx == first
    return m, hit, jnp.where(hit, -jnp.inf, vals)


def _peer_route_kernel(x_ref, g_ref, wq_ref, sk_ref, cidx_ref, cvalid_ref,
                       xnt_ref, nsel_ref, a_ref, rb_ref, b_ref,
                       q_scr, s_scr, rank_scr, sorted_scr):
    xn = _rmsnorm(x_ref[...], g_ref[...])
    xnt = xn.T.astype(BF16)
    xnt_ref[...] = xnt
    q_scr[...] = jnp.dot(wq_ref[...], xnt, preferred_element_type=F32).astype(BF16)
    key_idx = lax.broadcasted_iota(jnp.int32, (N_KEYS, LANES), 0).astype(F32)
    n_lane_tiles = ROUTE_TILE // LANES

    def stage1(hp, carry):
        qs = q_scr[pl.ds(pl.multiple_of(hp * D_KEY_HALF, D_KEY_HALF), D_KEY_HALF), :]
        s_all = jnp.dot(sk_ref[hp], qs, preferred_element_type=F32)
        s_scr[hp] = s_all
        for lt in range(n_lane_tiles):
            lanes = slice(lt * LANES, (lt + 1) * LANES)
            vals = s_all[:, lanes]
            rank = jnp.full((N_KEYS, LANES), NOT_SELECTED_RANK, F32)
            for it in range(PEER_TOPK):
                m, hit, vals = _extract_max(vals, key_idx)
                rank = jnp.where(hit, jnp.float32(it), rank)
                sorted_scr[hp, it:it + 1, lanes] = m
            rank_scr[hp, :, lanes] = rank
        return carry

    lax.fori_loop(0, 2 * PEER_HEADS, stage1, 0)

    cidx = cidx_ref[...]
    cvalid = cvalid_ref[...] > 0.5

    def stage2(h, carry):
        for lt in range(n_lane_tiles):
            lanes = slice(lt * LANES, (lt + 1) * LANES)
            a_s = sorted_scr[2 * h, :, lanes]
            b_s = sorted_scr[2 * h + 1, :, lanes]
            main = [a_s[r:r + 1, :] + b_s for r in range(CAND_MAIN_ROWS)]
            tail = a_s[CAND_MAIN_ROWS:, :] + b_s[0:1, :]
            cand = jnp.where(cvalid, jnp.concatenate(main + [tail], axis=0), -jnp.inf)
            sel = jnp.zeros((CAND_ROWS, LANES), F32)
            top = a_s[0:1, :] + b_s[0:1, :]
            z = jnp.zeros((1, LANES), F32)
            for it in range(PEER_TOPK):
                m, hit, cand = _extract_max(cand, cidx)
                sel = jnp.where(hit, 1.0, sel)
                z = z + jnp.exp(m - top)
            rank_a = rank_scr[2 * h, :, lanes]
            nsel = jnp.zeros((N_KEYS, LANES), F32)
            for r in range(PEER_TOPK):
                if r < CAND_MAIN_ROWS:
                    n_r = jnp.sum(sel[r * PEER_TOPK:(r + 1) * PEER_TOPK, :], axis=0, keepdims=True)
                else:
                    row = CAND_MAIN_ROWS * PEER_TOPK + r - CAND_MAIN_ROWS
                    n_r = sel[row:row + 1, :]
                nsel = jnp.where(rank_a == jnp.float32(r), n_r, nsel)
            nsel_ref[h, :, lanes] = nsel
            a_ref[h, :, lanes] = jnp.exp(s_scr[2 * h, :, lanes] - a_s[0:1, :])
            rb_ref[h, :, lanes] = rank_scr[2 * h + 1, :, lanes]
            b_ref[h, :, lanes] = jnp.exp(s_scr[2 * h + 1, :, lanes] - b_s[0:1, :]) / z
        return carry

    lax.fori_loop(0, PEER_HEADS, stage2, 0)


def _candidate_tables():
    rows = np.arange(CAND_ROWS)
    main = rows < CAND_MAIN_ROWS * PEER_TOPK
    r = np.where(main, rows // PEER_TOPK, CAND_MAIN_ROWS + rows - CAND_MAIN_ROWS * PEER_TOPK)
    c = np.where(main, rows % PEER_TOPK, 0)
    idx = (r * PEER_TOPK + c).astype(np.float32)
    valid = ((r + 1) * (c + 1) <= PEER_TOPK).astype(np.float32)
    tile = lambda v: np.ascontiguousarray(np.broadcast_to(v[:, None], (CAND_ROWS, LANES)))
    return tile(idx), tile(valid)


def _peer_route(h_all, g, wq_t_bf16, sub_keys_bf16):
    cidx, cvalid = _candidate_tables()
    n_tiles = N_TOK // ROUTE_TILE
    per_head = jax.ShapeDtypeStruct((PEER_HEADS, N_KEYS, N_TOK), F32)
    head_spec = pl.BlockSpec((PEER_HEADS, N_KEYS, ROUTE_TILE), lambda i: (0, 0, i))
    return pl.pallas_call(
        _peer_route_kernel,
        out_shape=(jax.ShapeDtypeStruct((D_MODEL, N_TOK), BF16), per_head, per_head, per_head, per_head),
        grid=(n_tiles,),
        in_specs=[
            pl.BlockSpec((ROUTE_TILE, D_MODEL), lambda i: (i, 0)),
            pl.BlockSpec((1, D_MODEL), lambda i: (0, 0)),
            pl.BlockSpec((2 * PEER_HEADS * D_KEY_HALF, D_MODEL), lambda i: (0, 0)),
            pl.BlockSpec((2 * PEER_HEADS, N_KEYS, D_KEY_HALF), lambda i: (0, 0, 0)),
            pl.BlockSpec((CAND_ROWS, LANES), lambda i: (0, 0)),
            pl.BlockSpec((CAND_ROWS, LANES), lambda i: (0, 0)),
        ],
        out_specs=(pl.BlockSpec((D_MODEL, ROUTE_TILE), lambda i: (0, i)),
                   head_spec, head_spec, head_spec, head_spec),
        scratch_shapes=[
            pltpu.VMEM((2 * PEER_HEADS * D_KEY_HALF, ROUTE_TILE), BF16),
            pltpu.VMEM((2 * PEER_HEADS, N_KEYS, ROUTE_TILE), F32),
            pltpu.VMEM((2 * PEER_HEADS, N_KEYS, ROUTE_TILE), F32),
            pltpu.VMEM((2 * PEER_HEADS, PEER_TOPK, ROUTE_TILE), F32),
        ],
        compiler_params=_params(("parallel",)),
        name="peer_route",
    )(h_all, g.reshape(1, D_MODEL), wq_t_bf16, sub_keys_bf16, jnp.asarray(cidx), jnp.asarray(cvalid))


def _gelu(x):
    return 0.5 * x * (1.0 + lax.erf(x * np.float32(1.0 / math.sqrt(2.0))))


def _peer_dense_kernel(xnt_ref, u_ref, vt_ref, nsel_ref, a_ref, rb_ref, b_ref, res_ref, o_ref,
                       acc_scr, h_scr, z_scr):
    e = pl.program_id(1)

    @pl.when(e == 0)
    def _():
        acc_scr[...] = jnp.zeros_like(acc_scr)

    h_scr[...] = jnp.dot(u_ref[...], xnt_ref[...], preferred_element_type=F32)

    first_keys = pl.ds(pl.multiple_of(e * KEYS_PER_BLOCK, SUBLANES), KEYS_PER_BLOCK)
    for ii in range(KEYS_PER_BLOCK):
        rows = slice(ii * N_KEYS, (ii + 1) * N_KEYS)
        for lt in range(DENSE_TILE // LANES):
            lanes = slice(lt * LANES, (lt + 1) * LANES)
            w = jnp.zeros((N_KEYS, LANES), F32)
            for h in range(PEER_HEADS):
                n_row = nsel_ref[h, first_keys, lanes][ii:ii + 1, :]
                a_row = a_ref[h, first_keys, lanes][ii:ii + 1, :]
                w = w + a_row * jnp.where(rb_ref[h, :, lanes] < n_row, b_ref[h, :, lanes], 0.0)
            z_scr[rows, lanes] = (_gelu(h_scr[rows, lanes]) * w).astype(BF16)
    acc_scr[...] += jnp.dot(vt_ref[...], z_scr[...], preferred_element_type=F32)

    @pl.when(e == pl.num_programs(1) - 1)
    def _():
        o_ref[...] = res_ref[...] + acc_scr[...].T


def _peer_dense(xnt, u_bf16, vt_bf16, nsel, a, rb, b, res):
    head_spec = pl.BlockSpec((PEER_HEADS, N_KEYS, DENSE_TILE), lambda t, e: (0, 0, t))
    return pl.pallas_call(
        _peer_dense_kernel,
        out_shape=jax.ShapeDtypeStruct((N_TOK, D_MODEL), F32),
        grid=(N_TOK // DENSE_TILE, N_EXPERTS // EXPERT_BLOCK),
        in_specs=[
            pl.BlockSpec((D_MODEL, DENSE_TILE), lambda t, e: (0, t)),
            pl.BlockSpec((EXPERT_BLOCK, D_MODEL), lambda t, e: (e, 0)),
            pl.BlockSpec((D_MODEL, EXPERT_BLOCK), lambda t, e: (0, e)),
            head_spec, head_spec, head_spec, head_spec,
            pl.BlockSpec((DENSE_TILE, D_MODEL), lambda t, e: (t, 0)),
        ],
        out_specs=pl.BlockSpec((DENSE_TILE, D_MODEL), lambda t, e: (t, 0)),
        scratch_shapes=[
            pltpu.VMEM((D_MODEL, DENSE_TILE), F32),
            pltpu.VMEM((EXPERT_BLOCK, DENSE_TILE), F32),
            pltpu.VMEM((EXPERT_BLOCK, DENSE_TILE), BF16),
        ],
        compiler_params=_params(("parallel", "arbitrary")),
        name="peer_dense",
    )(xnt, u_bf16, vt_bf16, nsel, a, rb, b, res)


def _peer(h_all, g, w_q, sub_keys, u_tab, v_tab):
    wq_t = w_q.T.astype(BF16)
    sk = sub_keys.reshape(2 * PEER_HEADS, N_KEYS, D_KEY_HALF).astype(BF16)
    xnt, nsel, a, rb, b = _peer_route(h_all, g, wq_t, sk)
    return _peer_dense(xnt, u_tab.astype(BF16), v_tab.T.astype(BF16), nsel, a, rb, b, h_all)


def _attention_layer(h_all, g, w_qkv, sinks, w_o, cache_k, cache_v, bias_p, bias_s):
    qkv = _norm_matmul(h_all, g, w_qkv.astype(BF16))
    o_p = _attn_prompt(qkv, sinks, bias_p)

    hq = N_HEADS * HEAD_DIM
    qkv_s = qkv[N_PROMPT:].reshape(DEC_SEQ, DEC_BATCH, QKV_DIM)
    q_s = qkv_s[..., :hq].reshape(DEC_SEQ, DEC_BATCH, N_KV_HEADS, GROUP, HEAD_DIM)
    qg = q_s.transpose(1, 2, 0, 3, 4).reshape(DEC_BATCH, N_KV_HEADS, DEC_SEQ * GROUP, HEAD_DIM)
    k_new = qkv_s[..., hq:hq + KV_DIM].reshape(DEC_SEQ, DEC_BATCH, N_KV_HEADS, HEAD_DIM).transpose(1, 0, 2, 3)
    v_new = qkv_s[..., hq + KV_DIM:].reshape(DEC_SEQ, DEC_BATCH, N_KV_HEADS, HEAD_DIM).transpose(1, 0, 2, 3)
    kc = jnp.concatenate([cache_k, k_new], axis=1)
    vc = jnp.concatenate([cache_v, v_new], axis=1)
    pad = ((0, 0), (0, 0), (0, KEYS_SAMPLE_PAD - KEYS_SAMPLE), (0, 0))
    kc_g = jnp.pad(kc.transpose(0, 2, 1, 3), pad).astype(BF16)
    vc_g = jnp.pad(vc.transpose(0, 2, 1, 3), pad).astype(BF16)
    sink_s = jnp.broadcast_to(sinks.reshape(N_KV_HEADS, 1, GROUP), (N_KV_HEADS, DEC_SEQ, GROUP))
    sink_s = sink_s.reshape(N_KV_HEADS, DEC_SEQ * GROUP, 1)
    og = _attn_sample(qg.astype(BF16), kc_g, vc_g, bias_s, sink_s)
    o_s = og.reshape(DEC_BATCH, N_KV_HEADS, DEC_SEQ, GROUP, HEAD_DIM).transpose(2, 0, 1, 3, 4)
    o_s = o_s.reshape(N_SAMPLE, D_MODEL).astype(BF16)

    h_new = _matmul_residual(jnp.concatenate([o_p, o_s], axis=0), w_o.astype(BF16), h_all)

    k_p = qkv[:N_PROMPT, hq:hq + KV_DIM].reshape(BATCH, SEQ, N_KV_HEADS, HEAD_DIM)[:, SEQ - WINDOW:]
    v_p = qkv[:N_PROMPT, hq + KV_DIM:].reshape(BATCH, SEQ, N_KV_HEADS, HEAD_DIM)[:, SEQ - WINDOW:]
    return h_new, k_p, v_p, kc[:, -WINDOW:], vc[:, -WINDOW:]


def _conv_layer(h_all, g, w_in, conv_w, w_out, state):
    bch = _norm_matmul(h_all, g, w_in.astype(BF16))
    wout = w_out.astype(BF16)
    h_p, ulast = _conv_prompt(bch, conv_w, wout, h_all)
    state_tm = state.transpose(1, 0, 2).reshape((CONV_W - 1) * DEC_BATCH, D_CONV)
    h_s, u_s = _conv_sample(bch[N_PROMPT:], state_tm, conv_w, wout, h_all[N_PROMPT:])
    ulast = ulast.reshape(BATCH, TILES_PER_SEQ, SUBLANES, D_CONV)
    conv_p = ulast[:, -1, SUBLANES - (CONV_W - 1):, :]
    conv_s = u_s.reshape(DEC_SEQ, DEC_BATCH, D_CONV)[DEC_SEQ - (CONV_W - 1):].transpose(1, 0, 2)
    return jnp.concatenate([h_p, h_s], axis=0), conv_p, conv_s


def _bias_tables(rel_bias):
    bias_p = _bias_table(rel_bias, _prompt_buckets())
    sb = _bias_table(rel_bias, np.pad(_sample_buckets(), ((0, SUBLANES - DEC_SEQ), (0, 0)), constant_values=-1))
    sb = sb[:, :DEC_SEQ, :]
    bias_s = sb.reshape(N_KV_HEADS, GROUP, DEC_SEQ, KEYS_SAMPLE_PAD).transpose(0, 2, 1, 3)
    return bias_p, bias_s.reshape(N_KV_HEADS, DEC_SEQ * GROUP, KEYS_SAMPLE_PAD)


def kernel(x_prompt, x_sample, cache_k, cache_v, state_conv, norm_mix_g, norm_ffn_g, norm_final_g, rel_bias, attn_w_qkv, attn_sinks, attn_w_o, conv_w_in, conv_w, conv_w_out, peer_w_q, peer_sub_keys, peer_u, peer_v):
    h_all = jnp.concatenate(
        [x_prompt.reshape(N_PROMPT, D_MODEL), x_sample.transpose(1, 0, 2).reshape(N_SAMPLE, D_MODEL)], axis=0)
    bias_p, bias_s = _bias_tables(rel_bias)
    kp_l, vp_l, cp_l, ks_l, vs_l, cs_l = [], [], [], [], [], []
    for i in range(DEPTH):
        j = i // 2
        if i % 2 == 0:
            h_all, kp, vp, kn, vn = _attention_layer(
                h_all, norm_mix_g[i], attn_w_qkv[j], attn_sinks[j], attn_w_o[j], cache_k[j], cache_v[j],
                bias_p, bias_s)
            kp_l.append(kp); vp_l.append(vp); ks_l.append(kn); vs_l.append(vn)
        else:
            h_all, cp, cn = _conv_layer(h_all, norm_mix_g[i], conv_w_in[j], conv_w[j], conv_w_out[j], state_conv[j])
            cp_l.append(cp); cs_l.append(cn)
        h_all = _peer(h_all, norm_ffn_g[i], peer_w_q[i], peer_sub_keys[i], peer_u[i], peer_v[i])
    y = _final_norm(h_all, norm_final_g)
    y_prompt = y[:N_PROMPT].reshape(BATCH, SEQ, D_MODEL)
    y_sample = y[N_PROMPT:].reshape(DEC_SEQ, DEC_BATCH, D_MODEL).transpose(1, 0, 2)
    return (y_prompt, y_sample,
            jnp.stack(kp_l), jnp.stack(vp_l), jnp.stack(cp_l),
            jnp.stack(ks_l), jnp.stack(vs_l), jnp.stack(cs_l))
```

```python
import itertools
import math

import jax
import jax.numpy as jnp
import numpy as np
from jax import lax
from jax.experimental import pallas as pl
from jax.experimental.pallas import tpu as pltpu

F32 = jnp.float32
BF16 = jnp.bfloat16

D_MODEL = 1024
BATCH = 4
SEQ = 4096
DEPTH = 4
DEC_BATCH = 128
DEC_SEQ = 4
HEAD_DIM = 64
N_HEADS = 16
N_KV_HEADS = 4
GROUP = N_HEADS // N_KV_HEADS
QKV_DIM = (N_HEADS + 2 * N_KV_HEADS) * HEAD_DIM
KV_DIM = N_KV_HEADS * HEAD_DIM
WINDOW = 128
BLOCK = 128
ATTN_SCALE = HEAD_DIM ** -0.5
NEG = -1e30
NUM_BUCKETS = 32
MAX_DISTANCE = 128
D_CONV = D_MODEL
CONV_W = 3
N_KEYS = 128
N_EXPERTS = N_KEYS * N_KEYS
PEER_HEADS = 8
D_KEY_HALF = 128
PEER_TOPK = 16
EPS = 1e-6

N_PROMPT = BATCH * SEQ
N_SAMPLE = DEC_BATCH * DEC_SEQ
N_TOK = N_PROMPT + N_SAMPLE

LANES = 128
SUBLANES = 8
PACKED_ROWS = 16
VMEM_LIMIT_BYTES = 56 * 1024 * 1024

ROW_TILE = 512
N_ROW_TILES = N_TOK // ROW_TILE
PROMPT_ROW_TILES = N_PROMPT // ROW_TILE
TILES_PER_SEQ = SEQ // ROW_TILE
ROUTE_TILE = 256
DENSE_TILE = 512
EXPERT_BLOCK = 1024
KEYS_PER_BLOCK = EXPERT_BLOCK // N_KEYS
KEYS_PER_GATE_GROUP = 2
CHUNKS_PER_GATE_GROUP = 4
SAMPLE_SEQ_BLOCK = 8
KEYS_SAMPLE = WINDOW + DEC_SEQ
KEYS_SAMPLE_PAD = 2 * LANES
CAND_MAIN_ROWS = 8
CAND_ROWS = CAND_MAIN_ROWS * PEER_TOPK + (PEER_TOPK - CAND_MAIN_ROWS)
NOT_SELECTED_RANK = 127.0


def _params(semantics):
    return pltpu.CompilerParams(dimension_semantics=semantics, vmem_limit_bytes=VMEM_LIMIT_BYTES)


def _rmsnorm(x, g):
    ms = jnp.mean(x * x, axis=-1, keepdims=True)
    return x * lax.rsqrt(ms + EPS) * g


def _norm_matmul_kernel(x_ref, g_ref, w_ref, o_ref):
    xn = _rmsnorm(x_ref[...], g_ref[...]).astype(BF16)
    o_ref[...] = jnp.dot(xn, w_ref[...], preferred_element_type=F32)


def _norm_matmul(x, g, w_bf16):
    n_out = w_bf16.shape[1]
    return pl.pallas_call(
        _norm_matmul_kernel,
        out_shape=jax.ShapeDtypeStruct((N_TOK, n_out), F32),
        grid=(N_ROW_TILES,),
        in_specs=[
            pl.BlockSpec((ROW_TILE, D_MODEL), lambda i: (i, 0)),
            pl.BlockSpec((1, D_MODEL), lambda i: (0, 0)),
            pl.BlockSpec((D_MODEL, n_out), lambda i: (0, 0)),
        ],
        out_specs=pl.BlockSpec((ROW_TILE, n_out), lambda i: (i, 0)),
        compiler_params=_params(("parallel",)),
        name="norm_matmul",
    )(x, g.reshape(1, D_MODEL), w_bf16)


def _matmul_residual_kernel(a_ref, w_ref, r_ref, o_ref):
    o_ref[...] = r_ref[...] + jnp.dot(a_ref[...], w_ref[...], preferred_element_type=F32)


def _matmul_residual(a_bf16, w_bf16, res):
    return pl.pallas_call(
        _matmul_residual_kernel,
        out_shape=jax.ShapeDtypeStruct((N_TOK, D_MODEL), F32),
        grid=(N_ROW_TILES,),
        in_specs=[
            pl.BlockSpec((ROW_TILE, D_MODEL), lambda i: (i, 0)),
            pl.BlockSpec((D_MODEL, D_MODEL), lambda i: (0, 0)),
            pl.BlockSpec((ROW_TILE, D_MODEL), lambda i: (i, 0)),
        ],
        out_specs=pl.BlockSpec((ROW_TILE, D_MODEL), lambda i: (i, 0)),
        compiler_params=_params(("parallel",)),
        name="matmul_residual",
    )(a_bf16, w_bf16, res)


def _final_norm_kernel(x_ref, g_ref, o_ref):
    o_ref[...] = _rmsnorm(x_ref[...], g_ref[...])


def _final_norm(x, g):
    return pl.pallas_call(
        _final_norm_kernel,
        out_shape=jax.ShapeDtypeStruct((N_TOK, D_MODEL), F32),
        grid=(N_ROW_TILES,),
        in_specs=[
            pl.BlockSpec((ROW_TILE, D_MODEL), lambda i: (i, 0)),
            pl.BlockSpec((1, D_MODEL), lambda i: (0, 0)),
        ],
        out_specs=pl.BlockSpec((ROW_TILE, D_MODEL), lambda i: (i, 0)),
        compiler_params=_params(("parallel",)),
        name="final_norm",
    )(x, g.reshape(1, D_MODEL))


def _t5_bucket_np(dist):
    n = np.maximum(dist, 0)
    max_exact = NUM_BUCKETS // 2
    nf = np.maximum(n, 1).astype(np.float32)
    large = max_exact + (np.log(nf / np.float32(max_exact)) / np.float32(math.log(MAX_DISTANCE / max_exact))
                         * np.float32(NUM_BUCKETS - max_exact)).astype(np.int32)
    large = np.minimum(large, NUM_BUCKETS - 1)
    return np.where(n < max_exact, n, large).astype(np.int32)


def _bias_table_kernel(rel_ref, bucket_ref, o_ref):
    bucket = bucket_ref[...]
    for h in range(N_HEADS):
        acc = jnp.full(bucket.shape, NEG, F32)
        for b in range(NUM_BUCKETS):
            acc = jnp.where(bucket == b, rel_ref[b, h], acc)
        o_ref[h] = acc


def _bias_table(rel_bias, bucket_np):
    rows, cols = bucket_np.shape
    return pl.pallas_call(
        _bias_table_kernel,
        out_shape=jax.ShapeDtypeStruct((N_HEADS, rows, cols), F32),
        in_specs=[
            pl.BlockSpec(memory_space=pltpu.SMEM),
            pl.BlockSpec((rows, cols), lambda: (0, 0)),
        ],
        out_specs=pl.BlockSpec((N_HEADS, rows, cols), lambda: (0, 0, 0)),
        name="bias_table",
    )(rel_bias, jnp.asarray(bucket_np))


def _prompt_buckets():
    r = np.arange(BLOCK)[:, None]
    c = np.arange(2 * BLOCK)[None, :]
    dist = r - c + BLOCK
    valid = (dist >= 0) & (dist < WINDOW)
    rest = np.where(valid, _t5_bucket_np(dist), -1).astype(np.int32)
    first = np.where(c >= BLOCK, rest, -1).astype(np.int32)
    return np.concatenate([first, rest], axis=0)


def _sample_buckets():
    j = np.arange(DEC_SEQ)[:, None]
    c = np.arange(KEYS_SAMPLE_PAD)[None, :]
    dist = j + WINDOW - c
    valid = (dist >= 0) & (dist < WINDOW) & (c < KEYS_SAMPLE)
    return np.where(valid, _t5_bucket_np(dist), -1).astype(np.int32)


def _softmax_sink_pv(s_list, v_list, sink):
    m = sink
    for s in s_list:
        m = jnp.maximum(m, jnp.max(s, axis=-1, keepdims=True))
    denom = jnp.exp(sink - m)
    acc = None
    for s, v in zip(s_list, v_list):
        p = jnp.exp(s - m)
        denom = denom + jnp.sum(p, axis=-1, keepdims=True)
        pv = jnp.dot(p.astype(BF16), v, preferred_element_type=F32)
        acc = pv if acc is None else acc + pv
    return acc / denom


def _attn_prompt_kernel(sink_ref, q_ref, kp_ref, kc_ref, vp_ref, vc_ref, bias_ref, o_ref):
    kp = kp_ref[...].astype(BF16)
    kc = kc_ref[...].astype(BF16)
    vp = vp_ref[...].astype(BF16)
    vc = vc_ref[...].astype(BF16)
    outs = []
    for g in range(N_KV_HEADS):
        ksl = slice(g * HEAD_DIM, (g + 1) * HEAD_DIM)
        qg = jnp.concatenate(
            [q_ref[:, (g * GROUP + hh) * HEAD_DIM:(g * GROUP + hh + 1) * HEAD_DIM] for hh in range(GROUP)],
            axis=0).astype(BF16)
        bias = jnp.concatenate([bias_ref[g * GROUP + hh] for hh in range(GROUP)], axis=0)
        sink = jnp.concatenate(
            [jnp.full((BLOCK, 1), sink_ref[g * GROUP + hh], F32) for hh in range(GROUP)], axis=0)
        dn = (((1,), (1,)), ((), ()))
        s_prev = lax.dot_general(qg, kp[:, ksl], dn, preferred_element_type=F32) * ATTN_SCALE + bias[:, :BLOCK]
        s_cur = lax.dot_general(qg, kc[:, ksl], dn, preferred_element_type=F32) * ATTN_SCALE + bias[:, BLOCK:]
        og = _softmax_sink_pv([s_prev, s_cur], [vp[:, ksl], vc[:, ksl]], sink)
        for hh in range(GROUP):
            outs.append(og[hh * BLOCK:(hh + 1) * BLOCK])
    o_ref[...] = jnp.concatenate(outs, axis=1).astype(o_ref.dtype)


def _attn_prompt(qkv, sinks, bias_tab):
    nb = SEQ // BLOCK
    kcol = N_HEADS * HEAD_DIM // KV_DIM
    row = lambda b, n: b * nb + n
    prev = lambda b, n: b * nb + jnp.maximum(n - 1, 0)
    return pl.pallas_call(
        _attn_prompt_kernel,
        out_shape=jax.ShapeDtypeStruct((N_PROMPT, D_MODEL), BF16),
        grid=(BATCH, nb),
        in_specs=[
            pl.BlockSpec(memory_space=pltpu.SMEM),
            pl.BlockSpec((BLOCK, N_HEADS * HEAD_DIM), lambda b, n: (row(b, n), 0)),
            pl.BlockSpec((BLOCK, KV_DIM), lambda b, n: (prev(b, n), kcol)),
            pl.BlockSpec((BLOCK, KV_DIM), lambda b, n: (row(b, n), kcol)),
            pl.BlockSpec((BLOCK, KV_DIM), lambda b, n: (prev(b, n), kcol + 1)),
            pl.BlockSpec((BLOCK, KV_DIM), lambda b, n: (row(b, n), kcol + 1)),
            pl.BlockSpec((N_HEADS, BLOCK, 2 * BLOCK), lambda b, n: (0, jnp.minimum(n, 1), 0)),
        ],
        out_specs=pl.BlockSpec((BLOCK, D_MODEL), lambda b, n: (row(b, n), 0)),
        compiler_params=_params(("parallel", "parallel")),
        name="attn_prompt",
    )(sinks, qkv, qkv, qkv, qkv, qkv, bias_tab)


def _attn_sample_kernel(q_ref, k_ref, v_ref, bias_ref, sink_ref, o_ref):
    nb = SAMPLE_SEQ_BLOCK * N_KV_HEADS
    rows = DEC_SEQ * GROUP
    q = q_ref[...].reshape(nb, rows, HEAD_DIM)
    k = k_ref[...].reshape(nb, KEYS_SAMPLE_PAD, HEAD_DIM)
    v = v_ref[...].reshape(nb, KEYS_SAMPLE_PAD, HEAD_DIM)
    s = jnp.einsum("nqd,nkd->nqk", q, k, preferred_element_type=F32) * ATTN_SCALE
    s = s.reshape(SAMPLE_SEQ_BLOCK, N_KV_HEADS, rows, KEYS_SAMPLE_PAD) + bias_ref[...][None]
    sink = sink_ref[...][None]
    m = jnp.maximum(jnp.max(s, axis=-1, keepdims=True), sink)
    p = jnp.exp(s - m)
    denom = jnp.sum(p, axis=-1, keepdims=True) + jnp.exp(sink - m)
    pv = jnp.einsum("nqk,nkd->nqd", p.reshape(nb, rows, KEYS_SAMPLE_PAD).astype(BF16), v,
                    preferred_element_type=F32)
    o_ref[...] = pv.reshape(SAMPLE_SEQ_BLOCK, N_KV_HEADS, rows, HEAD_DIM) / denom


def _attn_sample(qg, kc, vc, bias_s, sink_s):
    rows = DEC_SEQ * GROUP
    blk = lambda shape: pl.BlockSpec((SAMPLE_SEQ_BLOCK,) + shape, lambda i: (i, 0, 0, 0))
    return pl.pallas_call(
        _attn_sample_kernel,
        out_shape=jax.ShapeDtypeStruct((DEC_BATCH, N_KV_HEADS, rows, HEAD_DIM), F32),
        grid=(DEC_BATCH // SAMPLE_SEQ_BLOCK,),
        in_specs=[
            blk((N_KV_HEADS, rows, HEAD_DIM)),
            blk((N_KV_HEADS, KEYS_SAMPLE_PAD, HEAD_DIM)),
            blk((N_KV_HEADS, KEYS_SAMPLE_PAD, HEAD_DIM)),
            pl.BlockSpec((N_KV_HEADS, rows, KEYS_SAMPLE_PAD), lambda i: (0, 0, 0)),
            pl.BlockSpec((N_KV_HEADS, rows, 1), lambda i: (0, 0, 0)),
        ],
        out_specs=blk((N_KV_HEADS, rows, HEAD_DIM)),
        compiler_params=_params(("parallel",)),
        name="attn_sample",
    )(qg, kc, vc, bias_s, sink_s)


def _conv_prompt_kernel(bch_ref, w_ref, wout_ref, r_ref, o_ref, ulast_ref, ubuf):
    i = pl.program_id(0)

    @pl.when(i % TILES_PER_SEQ == 0)
    def _():
        ubuf[0:SUBLANES, :] = jnp.zeros((SUBLANES, D_CONV), F32)

    u = bch_ref[:, D_CONV:2 * D_CONV] * bch_ref[:, 2 * D_CONV:]
    ubuf[SUBLANES:, :] = u
    y = (w_ref[0:1, :] * ubuf[SUBLANES - 2:SUBLANES - 2 + ROW_TILE, :]
         + w_ref[1:2, :] * ubuf[SUBLANES - 1:SUBLANES - 1 + ROW_TILE, :]
         + w_ref[2:3, :] * u)
    gated = (bch_ref[:, :D_CONV] * y).astype(BF16)
    o_ref[...] = r_ref[...] + jnp.dot(gated, wout_ref[...], preferred_element_type=F32)
    tail = u[ROW_TILE - SUBLANES:, :]
    ulast_ref[...] = tail
    ubuf[0:SUBLANES, :] = tail


def _conv_prompt(bch, conv_w, wout_bf16, res):
    return pl.pallas_call(
        _conv_prompt_kernel,
        out_shape=(jax.ShapeDtypeStruct((N_PROMPT, D_MODEL), F32),
                   jax.ShapeDtypeStruct((PROMPT_ROW_TILES * SUBLANES, D_CONV), F32)),
        grid=(PROMPT_ROW_TILES,),
        in_specs=[
            pl.BlockSpec((ROW_TILE, 3 * D_CONV), lambda i: (i, 0)),
            pl.BlockSpec((CONV_W, D_CONV), lambda i: (0, 0)),
            pl.BlockSpec((D_CONV, D_MODEL), lambda i: (0, 0)),
            pl.BlockSpec((ROW_TILE, D_MODEL), lambda i: (i, 0)),
        ],
        out_specs=(pl.BlockSpec((ROW_TILE, D_MODEL), lambda i: (i, 0)),
                   pl.BlockSpec((SUBLANES, D_CONV), lambda i: (i, 0))),
        scratch_shapes=[pltpu.VMEM((ROW_TILE + SUBLANES, D_CONV), F32)],
        compiler_params=_params(("arbitrary",)),
        name="conv_prompt",
    )(bch, conv_w, wout_bf16, res)


def _conv_sample_kernel(bch_ref, st_ref, w_ref, wout_ref, r_ref, o_ref, u_ref):
    u = bch_ref[:, D_CONV:2 * D_CONV] * bch_ref[:, 2 * D_CONV:]
    u_ref[...] = u
    up = [st_ref[0:DEC_BATCH, :], st_ref[DEC_BATCH:, :]] + [u[t * DEC_BATCH:(t + 1) * DEC_BATCH] for t in range(DEC_SEQ)]
    y = jnp.concatenate(
        [w_ref[0:1, :] * up[t] + w_ref[1:2, :] * up[t + 1] + w_ref[2:3, :] * up[t + 2] for t in range(DEC_SEQ)], axis=0)
    gated = (bch_ref[:, :D_CONV] * y).astype(BF16)
    o_ref[...] = r_ref[...] + jnp.dot(gated, wout_ref[...], preferred_element_type=F32)


def _conv_sample(bch_s, state_tm, conv_w, wout_bf16, res_s):
    full = lambda shape: pl.BlockSpec(shape, lambda: (0,) * len(shape))
    return pl.pallas_call(
        _conv_sample_kernel,
        out_shape=(jax.ShapeDtypeStruct((N_SAMPLE, D_MODEL), F32),
                   jax.ShapeDtypeStruct((N_SAMPLE, D_CONV), F32)),
        in_specs=[full((N_SAMPLE, 3 * D_CONV)), full(((CONV_W - 1) * DEC_BATCH, D_CONV)),
                  full((CONV_W, D_CONV)), full((D_CONV, D_MODEL)), full((N_SAMPLE, D_MODEL))],
        out_specs=(full((N_SAMPLE, D_MODEL)), full((N_SAMPLE, D_CONV))),
        compiler_params=pltpu.CompilerParams(vmem_limit_bytes=VMEM_LIMIT_BYTES),
        name="conv_sample",
    )(bch_s, state_tm, conv_w, wout_bf16, res_s)


def _extract_max(vals, idx):
    m = jnp.max(vals, axis=0, keepdims=True)
    first = jnp.min(jnp.where(vals == m, idx, jnp.float32(1e9)), axis=0, keepdims=True)
    hit = idx == first
    return m, hit, jnp.where(hit, -jnp.inf, vals)


def _bf16_high_bits(v):
    bits = pltpu.bitcast(v, jnp.uint32)
    return (bits + jnp.uint32(0x7FFF) + ((bits >> 16) & jnp.uint32(1))) & jnp.uint32(0xFFFF0000)


def _bf16_pair_words(v):
    high = _bf16_high_bits(v)
    return high | (high >> 16)


def _bf16_split_words(v):
    half = v.shape[0] // 2
    return _bf16_high_bits(v[half:]) | (_bf16_high_bits(v[:half]) >> 16)


def _peer_route_kernel(x_ref, g_ref, wq_ref, sk_ref, cidx_ref, cvalid_ref,
                       xnt_ref, nsel_ref, a_ref, rb_ref, b_ref,
                       q_scr, s_scr, rank_scr, sorted_scr):
    xn = _rmsnorm(x_ref[...], g_ref[...])
    xnt = xn.T.astype(BF16)
    xnt_ref[...] = xnt
    q_scr[...] = jnp.dot(wq_ref[...], xnt, preferred_element_type=F32).astype(BF16)
    key_idx = lax.broadcasted_iota(jnp.int32, (N_KEYS, LANES), 0).astype(F32)
    n_lane_tiles = ROUTE_TILE // LANES

    def stage1(hp, carry):
        qs = q_scr[pl.ds(pl.multiple_of(hp * D_KEY_HALF, D_KEY_HALF), D_KEY_HALF), :]
        s_all = jnp.dot(sk_ref[hp], qs, preferred_element_type=F32)
        s_scr[hp] = s_all
        for lt in range(n_lane_tiles):
            lanes = slice(lt * LANES, (lt + 1) * LANES)
            vals = s_all[:, lanes]
            rank = jnp.full((N_KEYS, LANES), NOT_SELECTED_RANK, F32)
            for it in range(PEER_TOPK):
                m, hit, vals = _extract_max(vals, key_idx)
                rank = jnp.where(hit, jnp.float32(it), rank)
                sorted_scr[hp, it:it + 1, lanes] = m
            rank_scr[hp, :, lanes] = rank
        return carry

    lax.fori_loop(0, 2 * PEER_HEADS, stage1, 0)

    cidx = cidx_ref[...]
    cvalid = cvalid_ref[...] > 0.5

    def stage2(h, carry):
        for lt in range(n_lane_tiles):
            lanes = slice(lt * LANES, (lt + 1) * LANES)
            a_s = sorted_scr[2 * h, :, lanes]
            b_s = sorted_scr[2 * h + 1, :, lanes]
            main = [a_s[r:r + 1, :] + b_s for r in range(CAND_MAIN_ROWS)]
            tail = a_s[CAND_MAIN_ROWS:, :] + b_s[0:1, :]
            cand = jnp.where(cvalid, jnp.concatenate(main + [tail], axis=0), -jnp.inf)
            sel = jnp.zeros((CAND_ROWS, LANES), F32)
            top = a_s[0:1, :] + b_s[0:1, :]
            z = jnp.zeros((1, LANES), F32)
            for it in range(PEER_TOPK):
                m, hit, cand = _extract_max(cand, cidx)
                sel = jnp.where(hit, 1.0, sel)
                z = z + jnp.exp(m - top)
            rank_a = rank_scr[2 * h, :, lanes]
            nsel = jnp.zeros((N_KEYS, LANES), F32)
            for r in range(PEER_TOPK):
                if r < CAND_MAIN_ROWS:
                    n_r = jnp.sum(sel[r * PEER_TOPK:(r + 1) * PEER_TOPK, :], axis=0, keepdims=True)
                else:
                    row = CAND_MAIN_ROWS * PEER_TOPK + r - CAND_MAIN_ROWS
                    n_r = sel[row:row + 1, :]
                nsel = jnp.where(rank_a == jnp.float32(r), n_r, nsel)
            nsel_ref[h, :, lanes] = _bf16_pair_words(nsel)
            a_ref[h, :, lanes] = _bf16_pair_words(jnp.exp(s_scr[2 * h, :, lanes] - a_s[0:1, :]))
            rb_ref[h, :, lanes] = _bf16_split_words(rank_scr[2 * h + 1, :, lanes])
            b_ref[h, :, lanes] = _bf16_split_words(jnp.exp(s_scr[2 * h + 1, :, lanes] - b_s[0:1, :]) / z)
        return carry

    lax.fori_loop(0, PEER_HEADS, stage2, 0)


def _candidate_tables():
    rows = np.arange(CAND_ROWS)
    main = rows < CAND_MAIN_ROWS * PEER_TOPK
    r = np.where(main, rows // PEER_TOPK, CAND_MAIN_ROWS + rows - CAND_MAIN_ROWS * PEER_TOPK)
    c = np.where(main, rows % PEER_TOPK, 0)
    idx = (r * PEER_TOPK + c).astype(np.float32)
    valid = ((r + 1) * (c + 1) <= PEER_TOPK).astype(np.float32)
    tile = lambda v: np.ascontiguousarray(np.broadcast_to(v[:, None], (CAND_ROWS, LANES)))
    return tile(idx), tile(valid)


def _peer_route(h_all, g, wq_t_bf16, sub_keys_bf16):
    cidx, cvalid = _candidate_tables()
    n_tiles = N_TOK // ROUTE_TILE
    words = jax.ShapeDtypeStruct((PEER_HEADS, N_KEYS, N_TOK), jnp.uint32)
    halves = jax.ShapeDtypeStruct((PEER_HEADS, N_KEYS // 2, N_TOK), jnp.uint32)
    head_spec = pl.BlockSpec((PEER_HEADS, N_KEYS, ROUTE_TILE), lambda i: (0, 0, i))
    half_spec = pl.BlockSpec((PEER_HEADS, N_KEYS // 2, ROUTE_TILE), lambda i: (0, 0, i))
    return pl.pallas_call(
        _peer_route_kernel,
        out_shape=(jax.ShapeDtypeStruct((D_MODEL, N_TOK), BF16), words, words, halves, halves),
        grid=(n_tiles,),
        in_specs=[
            pl.BlockSpec((ROUTE_TILE, D_MODEL), lambda i: (i, 0)),
            pl.BlockSpec((1, D_MODEL), lambda i: (0, 0)),
            pl.BlockSpec((2 * PEER_HEADS * D_KEY_HALF, D_MODEL), lambda i: (0, 0)),
            pl.BlockSpec((2 * PEER_HEADS, N_KEYS, D_KEY_HALF), lambda i: (0, 0, 0)),
            pl.BlockSpec((CAND_ROWS, LANES), lambda i: (0, 0)),
            pl.BlockSpec((CAND_ROWS, LANES), lambda i: (0, 0)),
        ],
        out_specs=(pl.BlockSpec((D_MODEL, ROUTE_TILE), lambda i: (0, i)),
                   head_spec, head_spec, half_spec, half_spec),
        scratch_shapes=[
            pltpu.VMEM((2 * PEER_HEADS * D_KEY_HALF, ROUTE_TILE), BF16),
            pltpu.VMEM((2 * PEER_HEADS, N_KEYS, ROUTE_TILE), F32),
            pltpu.VMEM((2 * PEER_HEADS, N_KEYS, ROUTE_TILE), F32),
            pltpu.VMEM((2 * PEER_HEADS, PEER_TOPK, ROUTE_TILE), F32),
        ],
        compiler_params=_params(("parallel",)),
        name="peer_route",
    )(h_all, g.reshape(1, D_MODEL), wq_t_bf16, sub_keys_bf16, jnp.asarray(cidx), jnp.asarray(cvalid))


def _gelu(x):
    return 0.5 * x * (1.0 + lax.erf(x * np.float32(1.0 / math.sqrt(2.0))))


def _packed_row_broadcast(words, row):
    return pltpu.bitcast(jnp.broadcast_to(words[row:row + 1, :], (SUBLANES, LANES)), BF16)


def _peer_dense_kernel(xnt_ref, u_ref, vt_ref, nsel_ref, a_ref, rb_ref, b_ref, res_ref, o_ref,
                       acc_scr, h_scr, w_scr, z_new_scr, z_prev_scr):
    e = pl.program_id(1)
    n_blocks = pl.num_programs(1) - 1

    @pl.when(e == 0)
    def _():
        acc_scr[...] = jnp.zeros_like(acc_scr)
        z_new_scr[...] = jnp.zeros_like(z_new_scr)

    @pl.when(e < n_blocks)
    def _():
        z_prev_scr[...] = z_new_scr[...]
        first_keys = pl.ds(pl.multiple_of(e * KEYS_PER_BLOCK, SUBLANES), KEYS_PER_BLOCK)
        n_chunks = N_KEYS // PACKED_ROWS
        zero = jnp.zeros((PACKED_ROWS, LANES), BF16)
        for i0 in range(0, KEYS_PER_BLOCK, KEYS_PER_GATE_GROUP):
            group = range(i0, i0 + KEYS_PER_GATE_GROUP)
            for lt, k0 in itertools.product(range(DENSE_TILE // LANES), range(0, n_chunks, CHUNKS_PER_GATE_GROUP)):
                lanes = slice(lt * LANES, (lt + 1) * LANES)
                chunks = range(k0, k0 + CHUNKS_PER_GATE_GROUP)
                w = {(ii, k): zero for ii in group for k in chunks}
                for h in range(PEER_HEADS):
                    n_words = nsel_ref[h, first_keys, lanes]
                    a_words = a_ref[h, first_keys, lanes]
                    n_b = {ii: _packed_row_broadcast(n_words, ii) for ii in group}
                    a_b = {ii: _packed_row_broadcast(a_words, ii) for ii in group}
                    for k in chunks:
                        second = slice(k * SUBLANES, (k + 1) * SUBLANES)
                        rank_b = pltpu.bitcast(rb_ref[h, second, lanes], BF16)
                        gate_b = pltpu.bitcast(b_ref[h, second, lanes], BF16)
                        for ii in group:
                            w[ii, k] = w[ii, k] + a_b[ii] * jnp.where(rank_b < n_b[ii], gate_b, zero)
                for ii in group:
                    for k in chunks:
                        words = slice((ii * n_chunks + k) * SUBLANES, (ii * n_chunks + k + 1) * SUBLANES)
                        w_scr[words, lanes] = pltpu.bitcast(w[ii, k], jnp.uint32)
        h_scr[...] = jnp.dot(u_ref[...], xnt_ref[...], preferred_element_type=F32)
        acc_scr[...] += jnp.dot(vt_ref[...], z_prev_scr[...], preferred_element_type=F32)
        for ii, m, lt in itertools.product(range(KEYS_PER_BLOCK), range(n_chunks // 2), range(DENSE_TILE // LANES)):
            lanes = slice(lt * LANES, (lt + 1) * LANES)
            first_word = (ii * n_chunks + 2 * m) * SUBLANES
            words = w_scr[first_word:first_word + PACKED_ROWS, lanes]
            gates = (pltpu.bitcast(words << 16, F32), pltpu.bitcast(words & jnp.uint32(0xFFFF0000), F32))
            for half, gate in enumerate(gates):
                row0 = ii * N_KEYS + half * (N_KEYS // 2) + m * PACKED_ROWS
                rows = slice(row0, row0 + PACKED_ROWS)
                z_new_scr[rows, lanes] = (_gelu(h_scr[rows, lanes]) * gate).astype(BF16)

    @pl.when(e == n_blocks)
    def _():
        z_prev_scr[...] = z_new_scr[...]
        acc = acc_scr[...] + jnp.dot(vt_ref[...], z_prev_scr[...], preferred_element_type=F32)
        o_ref[...] = res_ref[...] + acc.T


def _peer_dense(xnt, u_bf16, vt_bf16, nsel, a, rb, b, res):
    n_blocks = N_EXPERTS // EXPERT_BLOCK
    head_spec = pl.BlockSpec((PEER_HEADS, N_KEYS, DENSE_TILE), lambda t, e: (0, 0, t))
    half_spec = pl.BlockSpec((PEER_HEADS, N_KEYS // 2, DENSE_TILE), lambda t, e: (0, 0, t))
    return pl.pallas_call(
        _peer_dense_kernel,
        out_shape=jax.ShapeDtypeStruct((N_TOK, D_MODEL), F32),
        grid=(N_TOK // DENSE_TILE, n_blocks + 1),
        in_specs=[
            pl.BlockSpec((D_MODEL, DENSE_TILE), lambda t, e: (0, t)),
            pl.BlockSpec((EXPERT_BLOCK, D_MODEL), lambda t, e: (jnp.minimum(e, n_blocks - 1), 0)),
            pl.BlockSpec((D_MODEL, EXPERT_BLOCK), lambda t, e: (0, jnp.maximum(e - 1, 0))),
            head_spec, head_spec, half_spec, half_spec,
            pl.BlockSpec((DENSE_TILE, D_MODEL), lambda t, e: (t, 0)),
        ],
        out_specs=pl.BlockSpec((DENSE_TILE, D_MODEL), lambda t, e: (t, 0)),
        scratch_shapes=[
            pltpu.VMEM((D_MODEL, DENSE_TILE), F32),
            pltpu.VMEM((EXPERT_BLOCK, DENSE_TILE), F32),
            pltpu.VMEM((EXPERT_BLOCK // 2, DENSE_TILE), jnp.uint32),
            pltpu.VMEM((EXPERT_BLOCK, DENSE_TILE), BF16),
            pltpu.VMEM((EXPERT_BLOCK, DENSE_TILE), BF16),
        ],
        compiler_params=pltpu.CompilerParams(
            dimension_semantics=("parallel", "arbitrary"), vmem_limit_bytes=VMEM_LIMIT_BYTES),
        name="peer_dense",
    )(xnt, u_bf16, vt_bf16, nsel, a, rb, b, res)


def _peer(h_all, g, w_q, sub_keys, u_tab, v_tab):
    wq_t = w_q.T.astype(BF16)
    sk = sub_keys.reshape(2 * PEER_HEADS, N_KEYS, D_KEY_HALF).astype(BF16)
    xnt, nsel, a, rb, b = _peer_route(h_all, g, wq_t, sk)
    return _peer_dense(xnt, u_tab.astype(BF16), v_tab.T.astype(BF16), nsel, a, rb, b, h_all)


def _attention_layer(h_all, g, w_qkv, sinks, w_o, cache_k, cache_v, bias_p, bias_s):
    qkv = _norm_matmul(h_all, g, w_qkv.astype(BF16))
    o_p = _attn_prompt(qkv, sinks, bias_p)

    hq = N_HEADS * HEAD_DIM
    qkv_s = qkv[N_PROMPT:].reshape(DEC_SEQ, DEC_BATCH, QKV_DIM)
    q_s = qkv_s[..., :hq].reshape(DEC_SEQ, DEC_BATCH, N_KV_HEADS, GROUP, HEAD_DIM)
    qg = q_s.transpose(1, 2, 0, 3, 4).reshape(DEC_BATCH, N_KV_HEADS, DEC_SEQ * GROUP, HEAD_DIM)
    k_new = qkv_s[..., hq:hq + KV_DIM].reshape(DEC_SEQ, DEC_BATCH, N_KV_HEADS, HEAD_DIM).transpose(1, 0, 2, 3)
    v_new = qkv_s[..., hq + KV_DIM:].reshape(DEC_SEQ, DEC_BATCH, N_KV_HEADS, HEAD_DIM).transpose(1, 0, 2, 3)
    kc = jnp.concatenate([cache_k, k_new], axis=1)
    vc = jnp.concatenate([cache_v, v_new], axis=1)
    pad = ((0, 0), (0, 0), (0, KEYS_SAMPLE_PAD - KEYS_SAMPLE), (0, 0))
    kc_g = jnp.pad(kc.transpose(0, 2, 1, 3), pad).astype(BF16)
    vc_g = jnp.pad(vc.transpose(0, 2, 1, 3), pad).astype(BF16)
    sink_s = jnp.broadcast_to(sinks.reshape(N_KV_HEADS, 1, GROUP), (N_KV_HEADS, DEC_SEQ, GROUP))
    sink_s = sink_s.reshape(N_KV_HEADS, DEC_SEQ * GROUP, 1)
    og = _attn_sample(qg.astype(BF16), kc_g, vc_g, bias_s, sink_s)
    o_s = og.reshape(DEC_BATCH, N_KV_HEADS, DEC_SEQ, GROUP, HEAD_DIM).transpose(2, 0, 1, 3, 4)
    o_s = o_s.reshape(N_SAMPLE, D_MODEL).astype(BF16)

    h_new = _matmul_residual(jnp.concatenate([o_p, o_s], axis=0), w_o.astype(BF16), h_all)

    k_p = qkv[:N_PROMPT, hq:hq + KV_DIM].reshape(BATCH, SEQ, N_KV_HEADS, HEAD_DIM)[:, SEQ - WINDOW:]
    v_p = qkv[:N_PROMPT, hq + KV_DIM:].reshape(BATCH, SEQ, N_KV_HEADS, HEAD_DIM)[:, SEQ - WINDOW:]
    return h_new, k_p, v_p, kc[:, -WINDOW:], vc[:, -WINDOW:]


def _conv_layer(h_all, g, w_in, conv_w, w_out, state):
    bch = _norm_matmul(h_all, g, w_in.astype(BF16))
    wout = w_out.astype(BF16)
    h_p, ulast = _conv_prompt(bch, conv_w, wout, h_all)
    state_tm = state.transpose(1, 0, 2).reshape((CONV_W - 1) * DEC_BATCH, D_CONV)
    h_s, u_s = _conv_sample(bch[N_PROMPT:], state_tm, conv_w, wout, h_all[N_PROMPT:])
    ulast = ulast.reshape(BATCH, TILES_PER_SEQ, SUBLANES, D_CONV)
    conv_p = ulast[:, -1, SUBLANES - (CONV_W - 1):, :]
    conv_s = u_s.reshape(DEC_SEQ, DEC_BATCH, D_CONV)[DEC_SEQ - (CONV_W - 1):].transpose(1, 0, 2)
    return jnp.concatenate([h_p, h_s], axis=0), conv_p, conv_s


def _bias_tables(rel_bias):
    bias_p = _bias_table(rel_bias, _prompt_buckets())
    sb = _bias_table(rel_bias, np.pad(_sample_buckets(), ((0, SUBLANES - DEC_SEQ), (0, 0)), constant_values=-1))
    sb = sb[:, :DEC_SEQ, :]
    bias_s = sb.reshape(N_KV_HEADS, GROUP, DEC_SEQ, KEYS_SAMPLE_PAD).transpose(0, 2, 1, 3)
    return bias_p, bias_s.reshape(N_KV_HEADS, DEC_SEQ * GROUP, KEYS_SAMPLE_PAD)


def kernel(x_prompt, x_sample, cache_k, cache_v, state_conv, norm_mix_g, norm_ffn_g, norm_final_g, rel_bias, attn_w_qkv, attn_sinks, attn_w_o, conv_w_in, conv_w, conv_w_out, peer_w_q, peer_sub_keys, peer_u, peer_v):
    h_all = jnp.concatenate(
        [x_prompt.reshape(N_PROMPT, D_MODEL), x_sample.transpose(1, 0, 2).reshape(N_SAMPLE, D_MODEL)], axis=0)
    bias_p, bias_s = _bias_tables(rel_bias)
    kp_l, vp_l, cp_l, ks_l, vs_l, cs_l = [], [], [], [], [], []
    for i in range(DEPTH):
        j = i // 2
        if i % 2 == 0:
            h_all, kp, vp, kn, vn = _attention_layer(
                h_all, norm_mix_g[i], attn_w_qkv[j], attn_sinks[j], attn_w_o[j], cache_k[j], cache_v[j],
                bias_p, bias_s)
            kp_l.append(kp); vp_l.append(vp); ks_l.append(kn); vs_l.append(vn)
        else:
            h_all, cp, cn = _conv_layer(h_all, norm_mix_g[i], conv_w_in[j], conv_w[j], conv_w_out[j], state_conv[j])
            cp_l.append(cp); cs_l.append(cn)
        h_all = _peer(h_all, norm_ffn_g[i], peer_w_q[i], peer_sub_keys[i], peer_u[i], peer_v[i])
    y = _final_norm(h_all, norm_final_g)
    y_prompt = y[:N_PROMPT].reshape(BATCH, SEQ, D_MODEL)
    y_sample = y[N_PROMPT:].reshape(DEC_SEQ, DEC_BATCH, D_MODEL).transpose(1, 0, 2)
    return (y_prompt, y_sample,
            jnp.stack(kp_l), jnp.stack(vp_l), jnp.stack(cp_l),
            jnp.stack(ks_l), jnp.stack(vs_l), jnp.stack(cs_l))
```

```python
import itertools
import math

import jax
import jax.numpy as jnp
import numpy as np
from jax import lax
from jax.experimental import pallas as pl
from jax.experimental.pallas import tpu as pltpu

F32 = jnp.float32
BF16 = jnp.bfloat16

D_MODEL = 1024
BATCH = 4
SEQ = 4096
DEPTH = 4
DEC_BATCH = 128
DEC_SEQ = 4
HEAD_DIM = 64
N_HEADS = 16
N_KV_HEADS = 4
GROUP = N_HEADS // N_KV_HEADS
QKV_DIM = (N_HEADS + 2 * N_KV_HEADS) * HEAD_DIM
KV_DIM = N_KV_HEADS * HEAD_DIM
WINDOW = 128
BLOCK = 128
ATTN_SCALE = HEAD_DIM ** -0.5
NEG = -1e30
NUM_BUCKETS = 32
MAX_DISTANCE = 128
D_CONV = D_MODEL
CONV_W = 3
N_KEYS = 128
N_EXPERTS = N_KEYS * N_KEYS
PEER_HEADS = 8
D_KEY_HALF = 128
PEER_TOPK = 16
EPS = 1e-6

N_PROMPT = BATCH * SEQ
N_SAMPLE = DEC_BATCH * DEC_SEQ
N_TOK = N_PROMPT + N_SAMPLE

LANES = 128
SUBLANES = 8
PACKED_ROWS = 16
VMEM_LIMIT_BYTES = 56 * 1024 * 1024

ROW_TILE = 512
N_ROW_TILES = N_TOK // ROW_TILE
PROMPT_ROW_TILES = N_PROMPT // ROW_TILE
TILES_PER_SEQ = SEQ // ROW_TILE
ROUTE_TILE = 256
DENSE_TILE = 512
EXPERT_BLOCK = 1024
KEYS_PER_BLOCK = EXPERT_BLOCK // N_KEYS
KEYS_PER_GATE_GROUP = 2
CHUNKS_PER_GATE_GROUP = 4
SAMPLE_SEQ_BLOCK = 8
KEYS_SAMPLE = WINDOW + DEC_SEQ
KEYS_SAMPLE_PAD = 2 * LANES
CAND_MAIN_ROWS = 8
CAND_ROWS = CAND_MAIN_ROWS * PEER_TOPK + (PEER_TOPK - CAND_MAIN_ROWS)
NOT_SELECTED_RANK = 127.0


def _params(semantics):
    return pltpu.CompilerParams(dimension_semantics=semantics, vmem_limit_bytes=VMEM_LIMIT_BYTES)


def _rmsnorm(x, g):
    ms = jnp.mean(x * x, axis=-1, keepdims=True)
    return x * lax.rsqrt(ms + EPS) * g


def _norm_matmul_kernel(x_ref, g_ref, w_ref, o_ref):
    xn = _rmsnorm(x_ref[...], g_ref[...]).astype(BF16)
    o_ref[...] = jnp.dot(xn, w_ref[...], preferred_element_type=F32)


def _norm_matmul(x, g, w_bf16):
    n_out = w_bf16.shape[1]
    return pl.pallas_call(
        _norm_matmul_kernel,
        out_shape=jax.ShapeDtypeStruct((N_TOK, n_out), F32),
        grid=(N_ROW_TILES,),
        in_specs=[
            pl.BlockSpec((ROW_TILE, D_MODEL), lambda i: (i, 0)),
            pl.BlockSpec((1, D_MODEL), lambda i: (0, 0)),
            pl.BlockSpec((D_MODEL, n_out), lambda i: (0, 0)),
        ],
        out_specs=pl.BlockSpec((ROW_TILE, n_out), lambda i: (i, 0)),
        compiler_params=_params(("parallel",)),
        name="norm_matmul",
    )(x, g.reshape(1, D_MODEL), w_bf16)


def _matmul_residual_kernel(a_ref, w_ref, r_ref, o_ref):
    o_ref[...] = r_ref[...] + jnp.dot(a_ref[...], w_ref[...], preferred_element_type=F32)


def _matmul_residual(a_bf16, w_bf16, res):
    return pl.pallas_call(
        _matmul_residual_kernel,
        out_shape=jax.ShapeDtypeStruct((N_TOK, D_MODEL), F32),
        grid=(N_ROW_TILES,),
        in_specs=[
            pl.BlockSpec((ROW_TILE, D_MODEL), lambda i: (i, 0)),
            pl.BlockSpec((D_MODEL, D_MODEL), lambda i: (0, 0)),
            pl.BlockSpec((ROW_TILE, D_MODEL), lambda i: (i, 0)),
        ],
        out_specs=pl.BlockSpec((ROW_TILE, D_MODEL), lambda i: (i, 0)),
        compiler_params=_params(("parallel",)),
        name="matmul_residual",
    )(a_bf16, w_bf16, res)


def _final_norm_kernel(x_ref, g_ref, o_ref):
    o_ref[...] = _rmsnorm(x_ref[...], g_ref[...])


def _final_norm(x, g):
    return pl.pallas_call(
        _final_norm_kernel,
        out_shape=jax.ShapeDtypeStruct((N_TOK, D_MODEL), F32),
        grid=(N_ROW_TILES,),
        in_specs=[
            pl.BlockSpec((ROW_TILE, D_MODEL), lambda i: (i, 0)),
            pl.BlockSpec((1, D_MODEL), lambda i: (0, 0)),
        ],
        out_specs=pl.BlockSpec((ROW_TILE, D_MODEL), lambda i: (i, 0)),
        compiler_params=_params(("parallel",)),
        name="final_norm",
    )(x, g.reshape(1, D_MODEL))


def _t5_bucket_np(dist):
    n = np.maximum(dist, 0)
    max_exact = NUM_BUCKETS // 2
    nf = np.maximum(n, 1).astype(np.float32)
    large = max_exact + (np.log(nf / np.float32(max_exact)) / np.float32(math.log(MAX_DISTANCE / max_exact))
                         * np.float32(NUM_BUCKETS - max_exact)).astype(np.int32)
    large = np.minimum(large, NUM_BUCKETS - 1)
    return np.where(n < max_exact, n, large).astype(np.int32)


def _bias_table_kernel(rel_ref, bucket_ref, o_ref):
    bucket = bucket_ref[...]
    for h in range(N_HEADS):
        acc = jnp.full(bucket.shape, NEG, F32)
        for b in range(NUM_BUCKETS):
            acc = jnp.where(bucket == b, rel_ref[b, h], acc)
        o_ref[h] = acc


def _bias_table(rel_bias, bucket_np):
    rows, cols = bucket_np.shape
    return pl.pallas_call(
        _bias_table_kernel,
        out_shape=jax.ShapeDtypeStruct((N_HEADS, rows, cols), F32),
        in_specs=[
            pl.BlockSpec(memory_space=pltpu.SMEM),
            pl.BlockSpec((rows, cols), lambda: (0, 0)),
        ],
        out_specs=pl.BlockSpec((N_HEADS, rows, cols), lambda: (0, 0, 0)),
        name="bias_table",
    )(rel_bias, jnp.asarray(bucket_np))


def _prompt_buckets():
    r = np.arange(BLOCK)[:, None]
    c = np.arange(2 * BLOCK)[None, :]
    dist = r - c + BLOCK
    valid = (dist >= 0) & (dist < WINDOW)
    rest = np.where(valid, _t5_bucket_np(dist), -1).astype(np.int32)
    first = np.where(c >= BLOCK, rest, -1).astype(np.int32)
    return np.concatenate([first, rest], axis=0)


def _sample_buckets():
    j = np.arange(DEC_SEQ)[:, None]
    c = np.arange(KEYS_SAMPLE_PAD)[None, :]
    dist = j + WINDOW - c
    valid = (dist >= 0) & (dist < WINDOW) & (c < KEYS_SAMPLE)
    return np.where(valid, _t5_bucket_np(dist), -1).astype(np.int32)


def _softmax_sink_pv(s_list, v_list, sink):
    m = sink
    for s in s_list:
        m = jnp.maximum(m, jnp.max(s, axis=-1, keepdims=True))
    denom = jnp.exp(sink - m)
    acc = None
    for s, v in zip(s_list, v_list):
        p = jnp.exp(s - m)
        denom = denom + jnp.sum(p, axis=-1, keepdims=True)
        pv = jnp.dot(p.astype(BF16), v, preferred_element_type=F32)
        acc = pv if acc is None else acc + pv
    return acc / denom


def _attn_prompt_kernel(sink_ref, q_ref, kp_ref, kc_ref, vp_ref, vc_ref, bias_ref, o_ref):
    kp = kp_ref[...].astype(BF16)
    kc = kc_ref[...].astype(BF16)
    vp = vp_ref[...].astype(BF16)
    vc = vc_ref[...].astype(BF16)
    outs = []
    for g in range(N_KV_HEADS):
        ksl = slice(g * HEAD_DIM, (g + 1) * HEAD_DIM)
        qg = jnp.concatenate(
            [q_ref[:, (g * GROUP + hh) * HEAD_DIM:(g * GROUP + hh + 1) * HEAD_DIM] for hh in range(GROUP)],
            axis=0).astype(BF16)
        bias = jnp.concatenate([bias_ref[g * GROUP + hh] for hh in range(GROUP)], axis=0)
        sink = jnp.concatenate(
            [jnp.full((BLOCK, 1), sink_ref[g * GROUP + hh], F32) for hh in range(GROUP)], axis=0)
        dn = (((1,), (1,)), ((), ()))
        s_prev = lax.dot_general(qg, kp[:, ksl], dn, preferred_element_type=F32) * ATTN_SCALE + bias[:, :BLOCK]
        s_cur = lax.dot_general(qg, kc[:, ksl], dn, preferred_element_type=F32) * ATTN_SCALE + bias[:, BLOCK:]
        og = _softmax_sink_pv([s_prev, s_cur], [vp[:, ksl], vc[:, ksl]], sink)
        for hh in range(GROUP):
            outs.append(og[hh * BLOCK:(hh + 1) * BLOCK])
    o_ref[...] = jnp.concatenate(outs, axis=1).astype(o_ref.dtype)


def _attn_prompt(qkv, sinks, bias_tab):
    nb = SEQ // BLOCK
    kcol = N_HEADS * HEAD_DIM // KV_DIM
    row = lambda b, n: b * nb + n
    prev = lambda b, n: b * nb + jnp.maximum(n - 1, 0)
    return pl.pallas_call(
        _attn_prompt_kernel,
        out_shape=jax.ShapeDtypeStruct((N_PROMPT, D_MODEL), BF16),
        grid=(BATCH, nb),
        in_specs=[
            pl.BlockSpec(memory_space=pltpu.SMEM),
            pl.BlockSpec((BLOCK, N_HEADS * HEAD_DIM), lambda b, n: (row(b, n), 0)),
            pl.BlockSpec((BLOCK, KV_DIM), lambda b, n: (prev(b, n), kcol)),
            pl.BlockSpec((BLOCK, KV_DIM), lambda b, n: (row(b, n), kcol)),
            pl.BlockSpec((BLOCK, KV_DIM), lambda b, n: (prev(b, n), kcol + 1)),
            pl.BlockSpec((BLOCK, KV_DIM), lambda b, n: (row(b, n), kcol + 1)),
            pl.BlockSpec((N_HEADS, BLOCK, 2 * BLOCK), lambda b, n: (0, jnp.minimum(n, 1), 0)),
        ],
        out_specs=pl.BlockSpec((BLOCK, D_MODEL), lambda b, n: (row(b, n), 0)),
        compiler_params=_params(("parallel", "parallel")),
        name="attn_prompt",
    )(sinks, qkv, qkv, qkv, qkv, qkv, bias_tab)


def _attn_sample_kernel(q_ref, k_ref, v_ref, bias_ref, sink_ref, o_ref):
    nb = SAMPLE_SEQ_BLOCK * N_KV_HEADS
    rows = DEC_SEQ * GROUP
    q = q_ref[...].reshape(nb, rows, HEAD_DIM)
    k = k_ref[...].reshape(nb, KEYS_SAMPLE_PAD, HEAD_DIM)
    v = v_ref[...].reshape(nb, KEYS_SAMPLE_PAD, HEAD_DIM)
    s = jnp.einsum("nqd,nkd->nqk", q, k, preferred_element_type=F32) * ATTN_SCALE
    s = s.reshape(SAMPLE_SEQ_BLOCK, N_KV_HEADS, rows, KEYS_SAMPLE_PAD) + bias_ref[...][None]
    sink = sink_ref[...][None]
    m = jnp.maximum(jnp.max(s, axis=-1, keepdims=True), sink)
    p = jnp.exp(s - m)
    denom = jnp.sum(p, axis=-1, keepdims=True) + jnp.exp(sink - m)
    pv = jnp.einsum("nqk,nkd->nqd", p.reshape(nb, rows, KEYS_SAMPLE_PAD).astype(BF16), v,
                    preferred_element_type=F32)
    o_ref[...] = pv.reshape(SAMPLE_SEQ_BLOCK, N_KV_HEADS, rows, HEAD_DIM) / denom


def _attn_sample(qg, kc, vc, bias_s, sink_s):
    rows = DEC_SEQ * GROUP
    blk = lambda shape: pl.BlockSpec((SAMPLE_SEQ_BLOCK,) + shape, lambda i: (i, 0, 0, 0))
    return pl.pallas_call(
        _attn_sample_kernel,
        out_shape=jax.ShapeDtypeStruct((DEC_BATCH, N_KV_HEADS, rows, HEAD_DIM), F32),
        grid=(DEC_BATCH // SAMPLE_SEQ_BLOCK,),
        in_specs=[
            blk((N_KV_HEADS, rows, HEAD_DIM)),
            blk((N_KV_HEADS, KEYS_SAMPLE_PAD, HEAD_DIM)),
            blk((N_KV_HEADS, KEYS_SAMPLE_PAD, HEAD_DIM)),
            pl.BlockSpec((N_KV_HEADS, rows, KEYS_SAMPLE_PAD), lambda i: (0, 0, 0)),
            pl.BlockSpec((N_KV_HEADS, rows, 1), lambda i: (0, 0, 0)),
        ],
        out_specs=blk((N_KV_HEADS, rows, HEAD_DIM)),
        compiler_params=_params(("parallel",)),
        name="attn_sample",
    )(qg, kc, vc, bias_s, sink_s)


def _conv_prompt_kernel(bch_ref, w_ref, wout_ref, r_ref, o_ref, ulast_ref, ubuf):
    i = pl.program_id(0)

    @pl.when(i % TILES_PER_SEQ == 0)
    def _():
        ubuf[0:SUBLANES, :] = jnp.zeros((SUBLANES, D_CONV), F32)

    u = bch_ref[:, D_CONV:2 * D_CONV] * bch_ref[:, 2 * D_CONV:]
    ubuf[SUBLANES:, :] = u
    y = (w_ref[0:1, :] * ubuf[SUBLANES - 2:SUBLANES - 2 + ROW_TILE, :]
         + w_ref[1:2, :] * ubuf[SUBLANES - 1:SUBLANES - 1 + ROW_TILE, :]
         + w_ref[2:3, :] * u)
    gated = (bch_ref[:, :D_CONV] * y).astype(BF16)
    o_ref[...] = r_ref[...] + jnp.dot(gated, wout_ref[...], preferred_element_type=F32)
    tail = u[ROW_TILE - SUBLANES:, :]
    ulast_ref[...] = tail
    ubuf[0:SUBLANES, :] = tail


def _conv_prompt(bch, conv_w, wout_bf16, res):
    return pl.pallas_call(
        _conv_prompt_kernel,
        out_shape=(jax.ShapeDtypeStruct((N_PROMPT, D_MODEL), F32),
                   jax.ShapeDtypeStruct((PROMPT_ROW_TILES * SUBLANES, D_CONV), F32)),
        grid=(PROMPT_ROW_TILES,),
        in_specs=[
            pl.BlockSpec((ROW_TILE, 3 * D_CONV), lambda i: (i, 0)),
            pl.BlockSpec((CONV_W, D_CONV), lambda i: (0, 0)),
            pl.BlockSpec((D_CONV, D_MODEL), lambda i: (0, 0)),
            pl.BlockSpec((ROW_TILE, D_MODEL), lambda i: (i, 0)),
        ],
        out_specs=(pl.BlockSpec((ROW_TILE, D_MODEL), lambda i: (i, 0)),
                   pl.BlockSpec((SUBLANES, D_CONV), lambda i: (i, 0))),
        scratch_shapes=[pltpu.VMEM((ROW_TILE + SUBLANES, D_CONV), F32)],
        compiler_params=_params(("arbitrary",)),
        name="conv_prompt",
    )(bch, conv_w, wout_bf16, res)


def _conv_sample_kernel(bch_ref, st_ref, w_ref, wout_ref, r_ref, o_ref, u_ref):
    u = bch_ref[:, D_CONV:2 * D_CONV] * bch_ref[:, 2 * D_CONV:]
    u_ref[...] = u
    up = [st_ref[0:DEC_BATCH, :], st_ref[DEC_BATCH:, :]] + [u[t * DEC_BATCH:(t + 1) * DEC_BATCH] for t in range(DEC_SEQ)]
    y = jnp.concatenate(
        [w_ref[0:1, :] * up[t] + w_ref[1:2, :] * up[t + 1] + w_ref[2:3, :] * up[t + 2] for t in range(DEC_SEQ)], axis=0)
    gated = (bch_ref[:, :D_CONV] * y).astype(BF16)
    o_ref[...] = r_ref[...] + jnp.dot(gated, wout_ref[...], preferred_element_type=F32)


def _conv_sample(bch_s, state_tm, conv_w, wout_bf16, res_s):
    full = lambda shape: pl.BlockSpec(shape, lambda: (0,) * len(shape))
    return pl.pallas_call(
        _conv_sample_kernel,
        out_shape=(jax.ShapeDtypeStruct((N_SAMPLE, D_MODEL), F32),
                   jax.ShapeDtypeStruct((N_SAMPLE, D_CONV), F32)),
        in_specs=[full((N_SAMPLE, 3 * D_CONV)), full(((CONV_W - 1) * DEC_BATCH, D_CONV)),
                  full((CONV_W, D_CONV)), full((D_CONV, D_MODEL)), full((N_SAMPLE, D_MODEL))],
        out_specs=(full((N_SAMPLE, D_MODEL)), full((N_SAMPLE, D_CONV))),
        compiler_params=pltpu.CompilerParams(vmem_limit_bytes=VMEM_LIMIT_BYTES),
        name="conv_sample",
    )(bch_s, state_tm, conv_w, wout_bf16, res_s)


def _extract_max(vals, idx):
    m = jnp.max(vals, axis=0, keepdims=True)
    first = jnp.min(jnp.where(vals == m, idx, jnp.float32(1e9)), axis=0, keepdims=True)
    hit = idx == first
    return m, hit, jnp.where(hit, -jnp.inf, vals)


def _bf16_high_bits(v):
    bits = pltpu.bitcast(v, jnp.uint32)
    return (bits + jnp.uint32(0x7FFF) + ((bits >> 16) & jnp.uint32(1))) & jnp.uint32(0xFFFF0000)


def _bf16_pair_words(v):
    high = _bf16_high_bits(v)
    return high | (high >> 16)


def _bf16_split_words(v):
    half = v.shape[0] // 2
    return _bf16_high_bits(v[half:]) | (_bf16_high_bits(v[:half]) >> 16)


def _sort16_comparators():
    pairs = []

    def merge(lo, n, step):
        double = step * 2
        if double < n:
            merge(lo, n, double)
            merge(lo + step, n, double)
            for i in range(lo + step, lo + n - step, double):
                pairs.append((i, i + step))
        else:
            pairs.append((lo, lo + step))

    def sort(lo, n):
        if n > 1:
            half = n // 2
            sort(lo, half)
            sort(lo + half, half)
            merge(lo, n, 1)

    sort(0, PEER_TOPK)
    return pairs


_SORT16 = _sort16_comparators()


def _compare_exchange(x, i, j):
    x[i], x[j] = jnp.maximum(x[i], x[j]), jnp.minimum(x[i], x[j])


def _sorted_top16(x):
    x = list(x)
    for i, j in _SORT16:
        _compare_exchange(x, i, j)
    for shift in (4, 2, 1):
        y = [pltpu.roll(v, shift, axis=0) for v in x]
        x = [jnp.maximum(x[i], y[PEER_TOPK - 1 - i]) for i in range(PEER_TOPK)]
        d = PEER_TOPK // 2
        while d:
            for i in range(PEER_TOPK):
                if not i & d:
                    _compare_exchange(x, i, i + d)
            d //= 2
    return x


def _rank_among(top, v):
    rank = jnp.zeros(v.shape, F32)
    for r in range(PEER_TOPK):
        rank = jnp.where(top[r] > v, jnp.float32(r + 1), rank)
    return rank


_CAND_COLS = [PEER_TOPK // (r + 1) for r in range(PEER_TOPK)]


def _peer_route_kernel(x_ref, g_ref, wq_ref, sk_ref, cidx_ref, cvalid_ref,
                       xnt_ref, nsel_ref, a_ref, rb_ref, b_ref,
                       q_scr, s_scr, rank_scr, sorted_scr):
    xn = _rmsnorm(x_ref[...], g_ref[...])
    xnt = xn.T.astype(BF16)
    xnt_ref[...] = xnt
    q_scr[...] = jnp.dot(wq_ref[...], xnt, preferred_element_type=F32).astype(BF16)
    n_lane_tiles = ROUTE_TILE // LANES
    n_groups = N_KEYS // SUBLANES

    def scores(hp, carry):
        qs = q_scr[pl.ds(pl.multiple_of(hp * D_KEY_HALF, D_KEY_HALF), D_KEY_HALF), :]
        s_scr[hp] = jnp.dot(sk_ref[hp], qs, preferred_element_type=F32)
        return carry

    lax.fori_loop(0, 2 * PEER_HEADS, scores, 0)

    sub = lax.broadcasted_iota(jnp.int32, (SUBLANES, LANES), 0)

    def fast_head(h, tied):
        for lt in range(n_lane_tiles):
            lanes = slice(lt * LANES, (lt + 1) * LANES)
            tops, ranks, vals = [], [], []
            for p in range(2):
                v = [s_scr[2 * h + p, k * SUBLANES:(k + 1) * SUBLANES, lanes] for k in range(n_groups)]
                top = _sorted_top16(v)
                rank = [_rank_among(top, vk) for vk in v]
                members = sum(jnp.where(rk < PEER_TOPK, 1.0, 0.0) for rk in rank)
                members = jnp.sum(members, axis=0, keepdims=True)
                tied = jnp.maximum(tied, jnp.where(members != PEER_TOPK, 1.0, 0.0))
                for r in range(PEER_TOPK - 1):
                    tied = jnp.maximum(tied, jnp.where(top[r] == top[r + 1], 1.0, 0.0))
                tops.append(top)
                ranks.append(rank)
                vals.append(v)
            a_top, b_top = tops
            b_col = [jnp.where(sub == c, b_top[c], 0.0) for c in range(PEER_TOPK)]
            b_lo = sum(b_col[:SUBLANES])
            b_hi = sum(b_col[SUBLANES:])
            a_hi = sum(jnp.where(sub == r, a_top[SUBLANES + r], 0.0) for r in range(SUBLANES))
            cands = [a_top[0] + b_lo, a_top[0] + b_hi]
            for r in range(1, SUBLANES):
                cands.append(jnp.where(sub < _CAND_COLS[r], a_top[r] + b_lo, -jnp.inf))
            cands.append(a_hi + b_top[0])
            minus_inf = jnp.full((SUBLANES, LANES), -jnp.inf, F32)
            sums = _sorted_top16(cands + [minus_inf] * (PEER_TOPK - len(cands)))
            z = sum(jnp.exp(sums[r] - sums[0]) for r in range(PEER_TOPK))
            picked = [jnp.where(c >= sums[PEER_TOPK - 1], 1.0, 0.0) for c in cands]
            counts = [jnp.sum(picked[0] + picked[1], axis=0, keepdims=True)]
            counts += [jnp.sum(picked[r + 1], axis=0, keepdims=True) for r in range(1, SUBLANES)]
            counts += [picked[SUBLANES + 1][r:r + 1, :] for r in range(SUBLANES)]
            tied = jnp.maximum(tied, jnp.where(sum(counts) != PEER_TOPK, 1.0, 0.0))
            nsel = []
            for rk in ranks[0]:
                n = jnp.zeros((SUBLANES, LANES), F32)
                for r in range(PEER_TOPK):
                    n = jnp.where(rk == r, counts[r], n)
                nsel.append(n)
            nsel_ref[h, :, lanes] = _bf16_pair_words(jnp.concatenate(nsel, axis=0))
            a_ref[h, :, lanes] = _bf16_pair_words(jnp.exp(jnp.concatenate(vals[0], axis=0) - a_top[0][0:1, :]))
            rb_ref[h, :, lanes] = _bf16_split_words(jnp.concatenate(ranks[1], axis=0))
            b_ref[h, :, lanes] = _bf16_split_words(
                jnp.exp(jnp.concatenate(vals[1], axis=0) - b_top[0][0:1, :]) / z[0:1, :])
        return tied

    tied = lax.fori_loop(0, PEER_HEADS, fast_head, jnp.zeros((SUBLANES, LANES), F32))

    @pl.when(jnp.max(tied) > 0.0)
    def _():
        _route_exact(cidx_ref, cvalid_ref, nsel_ref, a_ref, rb_ref, b_ref, s_scr, rank_scr, sorted_scr)


def _route_exact(cidx_ref, cvalid_ref, nsel_ref, a_ref, rb_ref, b_ref, s_scr, rank_scr, sorted_scr):
    key_idx = lax.broadcasted_iota(jnp.int32, (N_KEYS, LANES), 0).astype(F32)
    n_lane_tiles = ROUTE_TILE // LANES

    def stage1(hp, carry):
        for lt in range(n_lane_tiles):
            lanes = slice(lt * LANES, (lt + 1) * LANES)
            vals = s_scr[hp, :, lanes]
            rank = jnp.full((N_KEYS, LANES), NOT_SELECTED_RANK, F32)
            for it in range(PEER_TOPK):
                m, hit, vals = _extract_max(vals, key_idx)
                rank = jnp.where(hit, jnp.float32(it), rank)
                sorted_scr[hp, it:it + 1, lanes] = m
            rank_scr[hp, :, lanes] = rank
        return carry

    lax.fori_loop(0, 2 * PEER_HEADS, stage1, 0)

    cidx = cidx_ref[...]
    cvalid = cvalid_ref[...] > 0.5

    def stage2(h, carry):
        for lt in range(n_lane_tiles):
            lanes = slice(lt * LANES, (lt + 1) * LANES)
            a_s = sorted_scr[2 * h, :, lanes]
            b_s = sorted_scr[2 * h + 1, :, lanes]
            main = [a_s[r:r + 1, :] + b_s for r in range(CAND_MAIN_ROWS)]
            tail = a_s[CAND_MAIN_ROWS:, :] + b_s[0:1, :]
            cand = jnp.where(cvalid, jnp.concatenate(main + [tail], axis=0), -jnp.inf)
            sel = jnp.zeros((CAND_ROWS, LANES), F32)
            top = a_s[0:1, :] + b_s[0:1, :]
            z = jnp.zeros((1, LANES), F32)
            for it in range(PEER_TOPK):
                m, hit, cand = _extract_max(cand, cidx)
                sel = jnp.where(hit, 1.0, sel)
                z = z + jnp.exp(m - top)
            rank_a = rank_scr[2 * h, :, lanes]
            nsel = jnp.zeros((N_KEYS, LANES), F32)
            for r in range(PEER_TOPK):
                if r < CAND_MAIN_ROWS:
                    n_r = jnp.sum(sel[r * PEER_TOPK:(r + 1) * PEER_TOPK, :], axis=0, keepdims=True)
                else:
                    row = CAND_MAIN_ROWS * PEER_TOPK + r - CAND_MAIN_ROWS
                    n_r = sel[row:row + 1, :]
                nsel = jnp.where(rank_a == jnp.float32(r), n_r, nsel)
            nsel_ref[h, :, lanes] = _bf16_pair_words(nsel)
            a_ref[h, :, lanes] = _bf16_pair_words(jnp.exp(s_scr[2 * h, :, lanes] - a_s[0:1, :]))
            rb_ref[h, :, lanes] = _bf16_split_words(rank_scr[2 * h + 1, :, lanes])
            b_ref[h, :, lanes] = _bf16_split_words(jnp.exp(s_scr[2 * h + 1, :, lanes] - b_s[0:1, :]) / z)
        return carry

    lax.fori_loop(0, PEER_HEADS, stage2, 0)


def _candidate_tables():
    rows = np.arange(CAND_ROWS)
    main = rows < CAND_MAIN_ROWS * PEER_TOPK
    r = np.where(main, rows // PEER_TOPK, CAND_MAIN_ROWS + rows - CAND_MAIN_ROWS * PEER_TOPK)
    c = np.where(main, rows % PEER_TOPK, 0)
    idx = (r * PEER_TOPK + c).astype(np.float32)
    valid = ((r + 1) * (c + 1) <= PEER_TOPK).astype(np.float32)
    tile = lambda v: np.ascontiguousarray(np.broadcast_to(v[:, None], (CAND_ROWS, LANES)))
    return tile(idx), tile(valid)


def _peer_route(h_all, g, wq_t_bf16, sub_keys_bf16):
    cidx, cvalid = _candidate_tables()
    n_tiles = N_TOK // ROUTE_TILE
    words = jax.ShapeDtypeStruct((PEER_HEADS, N_KEYS, N_TOK), jnp.uint32)
    halves = jax.ShapeDtypeStruct((PEER_HEADS, N_KEYS // 2, N_TOK), jnp.uint32)
    head_spec = pl.BlockSpec((PEER_HEADS, N_KEYS, ROUTE_TILE), lambda i: (0, 0, i))
    half_spec = pl.BlockSpec((PEER_HEADS, N_KEYS // 2, ROUTE_TILE), lambda i: (0, 0, i))
    return pl.pallas_call(
        _peer_route_kernel,
        out_shape=(jax.ShapeDtypeStruct((D_MODEL, N_TOK), BF16), words, words, halves, halves),
        grid=(n_tiles,),
        in_specs=[
            pl.BlockSpec((ROUTE_TILE, D_MODEL), lambda i: (i, 0)),
            pl.BlockSpec((1, D_MODEL), lambda i: (0, 0)),
            pl.BlockSpec((2 * PEER_HEADS * D_KEY_HALF, D_MODEL), lambda i: (0, 0)),
            pl.BlockSpec((2 * PEER_HEADS, N_KEYS, D_KEY_HALF), lambda i: (0, 0, 0)),
            pl.BlockSpec((CAND_ROWS, LANES), lambda i: (0, 0)),
            pl.BlockSpec((CAND_ROWS, LANES), lambda i: (0, 0)),
        ],
        out_specs=(pl.BlockSpec((D_MODEL, ROUTE_TILE), lambda i: (0, i)),
                   head_spec, head_spec, half_spec, half_spec),
        scratch_shapes=[
            pltpu.VMEM((2 * PEER_HEADS * D_KEY_HALF, ROUTE_TILE), BF16),
            pltpu.VMEM((2 * PEER_HEADS, N_KEYS, ROUTE_TILE), F32),
            pltpu.VMEM((2 * PEER_HEADS, N_KEYS, ROUTE_TILE), F32),
            pltpu.VMEM((2 * PEER_HEADS, PEER_TOPK, ROUTE_TILE), F32),
        ],
        compiler_params=_params(("parallel",)),
        name="peer_route",
    )(h_all, g.reshape(1, D_MODEL), wq_t_bf16, sub_keys_bf16, jnp.asarray(cidx), jnp.asarray(cvalid))


def _gelu(x):
    return 0.5 * x * (1.0 + lax.erf(x * np.float32(1.0 / math.sqrt(2.0))))


def _packed_row_broadcast(words, row):
    return pltpu.bitcast(jnp.broadcast_to(words[row:row + 1, :], (SUBLANES, LANES)), BF16)


def _peer_dense_kernel(xnt_ref, u_ref, vt_ref, nsel_ref, a_ref, rb_ref, b_ref, res_ref, o_ref,
                       acc_scr, h_scr, w_scr, z_new_scr, z_prev_scr):
    e = pl.program_id(1)
    n_blocks = pl.num_programs(1) - 1

    @pl.when(e == 0)
    def _():
        acc_scr[...] = jnp.zeros_like(acc_scr)
        z_new_scr[...] = jnp.zeros_like(z_new_scr)

    @pl.when(e < n_blocks)
    def _():
        z_prev_scr[...] = z_new_scr[...]
        first_keys = pl.ds(pl.multiple_of(e * KEYS_PER_BLOCK, SUBLANES), KEYS_PER_BLOCK)
        n_chunks = N_KEYS // PACKED_ROWS
        zero = jnp.zeros((PACKED_ROWS, LANES), BF16)
        for i0 in range(0, KEYS_PER_BLOCK, KEYS_PER_GATE_GROUP):
            group = range(i0, i0 + KEYS_PER_GATE_GROUP)
            for lt, k0 in itertools.product(range(DENSE_TILE // LANES), range(0, n_chunks, CHUNKS_PER_GATE_GROUP)):
                lanes = slice(lt * LANES, (lt + 1) * LANES)
                chunks = range(k0, k0 + CHUNKS_PER_GATE_GROUP)
                w = {(ii, k): zero for ii in group for k in chunks}
                for h in range(PEER_HEADS):
                    n_words = nsel_ref[h, first_keys, lanes]
                    a_words = a_ref[h, first_keys, lanes]
                    n_b = {ii: _packed_row_broadcast(n_words, ii) for ii in group}
                    a_b = {ii: _packed_row_broadcast(a_words, ii) for ii in group}
                    for k in chunks:
                        second = slice(k * SUBLANES, (k + 1) * SUBLANES)
                        rank_b = pltpu.bitcast(rb_ref[h, second, lanes], BF16)
                        gate_b = pltpu.bitcast(b_ref[h, second, lanes], BF16)
                        for ii in group:
                            w[ii, k] = w[ii, k] + a_b[ii] * jnp.where(rank_b < n_b[ii], gate_b, zero)
                for ii in group:
                    for k in chunks:
                        words = slice((ii * n_chunks + k) * SUBLANES, (ii * n_chunks + k + 1) * SUBLANES)
                        w_scr[words, lanes] = pltpu.bitcast(w[ii, k], jnp.uint32)
        h_scr[...] = jnp.dot(u_ref[...], xnt_ref[...], preferred_element_type=F32)
        acc_scr[...] += jnp.dot(vt_ref[...], z_prev_scr[...], preferred_element_type=F32)
        for ii, m, lt in itertools.product(range(KEYS_PER_BLOCK), range(n_chunks // 2), range(DENSE_TILE // LANES)):
            lanes = slice(lt * LANES, (lt + 1) * LANES)
            first_word = (ii * n_chunks + 2 * m) * SUBLANES
            words = w_scr[first_word:first_word + PACKED_ROWS, lanes]
            gates = (pltpu.bitcast(words << 16, F32), pltpu.bitcast(words & jnp.uint32(0xFFFF0000), F32))
            for half, gate in enumerate(gates):
                row0 = ii * N_KEYS + half * (N_KEYS // 2) + m * PACKED_ROWS
                rows = slice(row0, row0 + PACKED_ROWS)
                z_new_scr[rows, lanes] = (_gelu(h_scr[rows, lanes]) * gate).astype(BF16)

    @pl.when(e == n_blocks)
    def _():
        z_prev_scr[...] = z_new_scr[...]
        acc = acc_scr[...] + jnp.dot(vt_ref[...], z_prev_scr[...], preferred_element_type=F32)
        o_ref[...] = res_ref[...] + acc.T


def _peer_dense(xnt, u_bf16, vt_bf16, nsel, a, rb, b, res):
    n_blocks = N_EXPERTS // EXPERT_BLOCK
    head_spec = pl.BlockSpec((PEER_HEADS, N_KEYS, DENSE_TILE), lambda t, e: (0, 0, t))
    half_spec = pl.BlockSpec((PEER_HEADS, N_KEYS // 2, DENSE_TILE), lambda t, e: (0, 0, t))
    return pl.pallas_call(
        _peer_dense_kernel,
        out_shape=jax.ShapeDtypeStruct((N_TOK, D_MODEL), F32),
        grid=(N_TOK // DENSE_TILE, n_blocks + 1),
        in_specs=[
            pl.BlockSpec((D_MODEL, DENSE_TILE), lambda t, e: (0, t)),
            pl.BlockSpec((EXPERT_BLOCK, D_MODEL), lambda t, e: (jnp.minimum(e, n_blocks - 1), 0)),
            pl.BlockSpec((D_MODEL, EXPERT_BLOCK), lambda t, e: (0, jnp.maximum(e - 1, 0))),
            head_spec, head_spec, half_spec, half_spec,
            pl.BlockSpec((DENSE_TILE, D_MODEL), lambda t, e: (t, 0)),
        ],
        out_specs=pl.BlockSpec((DENSE_TILE, D_MODEL), lambda t, e: (t, 0)),
        scratch_shapes=[
            pltpu.VMEM((D_MODEL, DENSE_TILE), F32),
            pltpu.VMEM((EXPERT_BLOCK, DENSE_TILE), F32),
            pltpu.VMEM((EXPERT_BLOCK // 2, DENSE_TILE), jnp.uint32),
            pltpu.VMEM((EXPERT_BLOCK, DENSE_TILE), BF16),
            pltpu.VMEM((EXPERT_BLOCK, DENSE_TILE), BF16),
        ],
        compiler_params=pltpu.CompilerParams(
            dimension_semantics=("parallel", "arbitrary"), vmem_limit_bytes=VMEM_LIMIT_BYTES),
        name="peer_dense",
    )(xnt, u_bf16, vt_bf16, nsel, a, rb, b, res)


def _peer(h_all, g, w_q, sub_keys, u_tab, v_tab):
    wq_t = w_q.T.astype(BF16)
    sk = sub_keys.reshape(2 * PEER_HEADS, N_KEYS, D_KEY_HALF).astype(BF16)
    xnt, nsel, a, rb, b = _peer_route(h_all, g, wq_t, sk)
    return _peer_dense(xnt, u_tab.astype(BF16), v_tab.T.astype(BF16), nsel, a, rb, b, h_all)


def _attention_layer(h_all, g, w_qkv, sinks, w_o, cache_k, cache_v, bias_p, bias_s):
    qkv = _norm_matmul(h_all, g, w_qkv.astype(BF16))
    o_p = _attn_prompt(qkv, sinks, bias_p)

    hq = N_HEADS * HEAD_DIM
    qkv_s = qkv[N_PROMPT:].reshape(DEC_SEQ, DEC_BATCH, QKV_DIM)
    q_s = qkv_s[..., :hq].reshape(DEC_SEQ, DEC_BATCH, N_KV_HEADS, GROUP, HEAD_DIM)
    qg = q_s.transpose(1, 2, 0, 3, 4).reshape(DEC_BATCH, N_KV_HEADS, DEC_SEQ * GROUP, HEAD_DIM)
    k_new = qkv_s[..., hq:hq + KV_DIM].reshape(DEC_SEQ, DEC_BATCH, N_KV_HEADS, HEAD_DIM).transpose(1, 0, 2, 3)
    v_new = qkv_s[..., hq + KV_DIM:].reshape(DEC_SEQ, DEC_BATCH, N_KV_HEADS, HEAD_DIM).transpose(1, 0, 2, 3)
    kc = jnp.concatenate([cache_k, k_new], axis=1)
    vc = jnp.concatenate([cache_v, v_new], axis=1)
    pad = ((0, 0), (0, 0), (0, KEYS_SAMPLE_PAD - KEYS_SAMPLE), (0, 0))
    kc_g = jnp.pad(kc.transpose(0, 2, 1, 3), pad).astype(BF16)
    vc_g = jnp.pad(vc.transpose(0, 2, 1, 3), pad).astype(BF16)
    sink_s = jnp.broadcast_to(sinks.reshape(N_KV_HEADS, 1, GROUP), (N_KV_HEADS, DEC_SEQ, GROUP))
    sink_s = sink_s.reshape(N_KV_HEADS, DEC_SEQ * GROUP, 1)
    og = _attn_sample(qg.astype(BF16), kc_g, vc_g, bias_s, sink_s)
    o_s = og.reshape(DEC_BATCH, N_KV_HEADS, DEC_SEQ, GROUP, HEAD_DIM).transpose(2, 0, 1, 3, 4)
    o_s = o_s.reshape(N_SAMPLE, D_MODEL).astype(BF16)

    h_new = _matmul_residual(jnp.concatenate([o_p, o_s], axis=0), w_o.astype(BF16), h_all)

    k_p = qkv[:N_PROMPT, hq:hq + KV_DIM].reshape(BATCH, SEQ, N_KV_HEADS, HEAD_DIM)[:, SEQ - WINDOW:]
    v_p = qkv[:N_PROMPT, hq + KV_DIM:].reshape(BATCH, SEQ, N_KV_HEADS, HEAD_DIM)[:, SEQ - WINDOW:]
    return h_new, k_p, v_p, kc[:, -WINDOW:], vc[:, -WINDOW:]


def _conv_layer(h_all, g, w_in, conv_w, w_out, state):
    bch = _norm_matmul(h_all, g, w_in.astype(BF16))
    wout = w_out.astype(BF16)
    h_p, ulast = _conv_prompt(bch, conv_w, wout, h_all)
    state_tm = state.transpose(1, 0, 2).reshape((CONV_W - 1) * DEC_BATCH, D_CONV)
    h_s, u_s = _conv_sample(bch[N_PROMPT:], state_tm, conv_w, wout, h_all[N_PROMPT:])
    ulast = ulast.reshape(BATCH, TILES_PER_SEQ, SUBLANES, D_CONV)
    conv_p = ulast[:, -1, SUBLANES - (CONV_W - 1):, :]
    conv_s = u_s.reshape(DEC_SEQ, DEC_BATCH, D_CONV)[DEC_SEQ - (CONV_W - 1):].transpose(1, 0, 2)
    return jnp.concatenate([h_p, h_s], axis=0), conv_p, conv_s


def _bias_tables(rel_bias):
    bias_p = _bias_table(rel_bias, _prompt_buckets())
    sb = _bias_table(rel_bias, np.pad(_sample_buckets(), ((0, SUBLANES - DEC_SEQ), (0, 0)), constant_values=-1))
    sb = sb[:, :DEC_SEQ, :]
    bias_s = sb.reshape(N_KV_HEADS, GROUP, DEC_SEQ, KEYS_SAMPLE_PAD).transpose(0, 2, 1, 3)
    return bias_p, bias_s.reshape(N_KV_HEADS, DEC_SEQ * GROUP, KEYS_SAMPLE_PAD)


def kernel(x_prompt, x_sample, cache_k, cache_v, state_conv, norm_mix_g, norm_ffn_g, norm_final_g, rel_bias, attn_w_qkv, attn_sinks, attn_w_o, conv_w_in, conv_w, conv_w_out, peer_w_q, peer_sub_keys, peer_u, peer_v):
    h_all = jnp.concatenate(
        [x_prompt.reshape(N_PROMPT, D_MODEL), x_sample.transpose(1, 0, 2).reshape(N_SAMPLE, D_MODEL)], axis=0)
    bias_p, bias_s = _bias_tables(rel_bias)
    kp_l, vp_l, cp_l, ks_l, vs_l, cs_l = [], [], [], [], [], []
    for i in range(DEPTH):
        j = i // 2
        if i % 2 == 0:
            h_all, kp, vp, kn, vn = _attention_layer(
                h_all, norm_mix_g[i], attn_w_qkv[j], attn_sinks[j], attn_w_o[j], cache_k[j], cache_v[j],
                bias_p, bias_s)
            kp_l.append(kp); vp_l.append(vp); ks_l.append(kn); vs_l.append(vn)
        else:
            h_all, cp, cn = _conv_layer(h_all, norm_mix_g[i], conv_w_in[j], conv_w[j], conv_w_out[j], state_conv[j])
            cp_l.append(cp); cs_l.append(cn)
        h_all = _peer(h_all, norm_ffn_g[i], peer_w_q[i], peer_sub_keys[i], peer_u[i], peer_v[i])
    y = _final_norm(h_all, norm_final_g)
    y_prompt = y[:N_PROMPT].reshape(BATCH, SEQ, D_MODEL)
    y_sample = y[N_PROMPT:].reshape(DEC_SEQ, DEC_BATCH, D_MODEL).transpose(1, 0, 2)
    return (y_prompt, y_sample,
            jnp.stack(kp_l), jnp.stack(vp_l), jnp.stack(cp_l),
            jnp.stack(ks_l), jnp.stack(vs_l), jnp.stack(cs_l))
```

```python
import itertools
import math

import jax
import jax.numpy as jnp
import numpy as np
from jax import lax
from jax.experimental import pallas as pl
from jax.experimental.pallas import tpu as pltpu

F32 = jnp.float32
BF16 = jnp.bfloat16

D_MODEL = 1024
BATCH = 4
SEQ = 4096
DEPTH = 4
DEC_BATCH = 128
DEC_SEQ = 4
HEAD_DIM = 64
N_HEADS = 16
N_KV_HEADS = 4
GROUP = N_HEADS // N_KV_HEADS
QKV_DIM = (N_HEADS + 2 * N_KV_HEADS) * HEAD_DIM
KV_DIM = N_KV_HEADS * HEAD_DIM
WINDOW = 128
BLOCK = 128
ATTN_SCALE = HEAD_DIM ** -0.5
NEG = -1e30
NUM_BUCKETS = 32
MAX_DISTANCE = 128
D_CONV = D_MODEL
CONV_W = 3
N_KEYS = 128
N_EXPERTS = N_KEYS * N_KEYS
PEER_HEADS = 8
D_KEY_HALF = 128
PEER_TOPK = 16
EPS = 1e-6

N_PROMPT = BATCH * SEQ
N_SAMPLE = DEC_BATCH * DEC_SEQ
N_TOK = N_PROMPT + N_SAMPLE

LANES = 128
SUBLANES = 8
PACKED_ROWS = 16
VMEM_LIMIT_BYTES = 56 * 1024 * 1024

ROW_TILE = 512
N_ROW_TILES = N_TOK // ROW_TILE
PROMPT_ROW_TILES = N_PROMPT // ROW_TILE
TILES_PER_SEQ = SEQ // ROW_TILE
ROUTE_TILE = 256
DENSE_TILE = 512
EXPERT_BLOCK = 1024
KEYS_PER_BLOCK = EXPERT_BLOCK // N_KEYS
KEYS_PER_GATE_GROUP = 2
CHUNKS_PER_GATE_GROUP = 4
SAMPLE_SEQ_BLOCK = 8
KEYS_SAMPLE = WINDOW + DEC_SEQ
KEYS_SAMPLE_PAD = 2 * LANES
CAND_MAIN_ROWS = 8
CAND_ROWS = CAND_MAIN_ROWS * PEER_TOPK + (PEER_TOPK - CAND_MAIN_ROWS)
NOT_SELECTED_RANK = 127.0


def _params(semantics):
    return pltpu.CompilerParams(dimension_semantics=semantics, vmem_limit_bytes=VMEM_LIMIT_BYTES)


def _rmsnorm(x, g):
    ms = jnp.mean(x * x, axis=-1, keepdims=True)
    return x * lax.rsqrt(ms + EPS) * g


def _norm_matmul_kernel(x_ref, g_ref, w_ref, o_ref):
    xn = _rmsnorm(x_ref[...], g_ref[...]).astype(BF16)
    o_ref[...] = jnp.dot(xn, w_ref[...], preferred_element_type=F32)


def _norm_matmul(x, g, w_bf16):
    n_out = w_bf16.shape[1]
    return pl.pallas_call(
        _norm_matmul_kernel,
        out_shape=jax.ShapeDtypeStruct((N_TOK, n_out), F32),
        grid=(N_ROW_TILES,),
        in_specs=[
            pl.BlockSpec((ROW_TILE, D_MODEL), lambda i: (i, 0)),
            pl.BlockSpec((1, D_MODEL), lambda i: (0, 0)),
            pl.BlockSpec((D_MODEL, n_out), lambda i: (0, 0)),
        ],
        out_specs=pl.BlockSpec((ROW_TILE, n_out), lambda i: (i, 0)),
        compiler_params=_params(("parallel",)),
        name="norm_matmul",
    )(x, g.reshape(1, D_MODEL), w_bf16)


def _matmul_residual_kernel(a_ref, w_ref, r_ref, o_ref):
    o_ref[...] = r_ref[...] + jnp.dot(a_ref[...], w_ref[...], preferred_element_type=F32)


def _matmul_residual(a_bf16, w_bf16, res):
    return pl.pallas_call(
        _matmul_residual_kernel,
        out_shape=jax.ShapeDtypeStruct((N_TOK, D_MODEL), F32),
        grid=(N_ROW_TILES,),
        in_specs=[
            pl.BlockSpec((ROW_TILE, D_MODEL), lambda i: (i, 0)),
            pl.BlockSpec((D_MODEL, D_MODEL), lambda i: (0, 0)),
            pl.BlockSpec((ROW_TILE, D_MODEL), lambda i: (i, 0)),
        ],
        out_specs=pl.BlockSpec((ROW_TILE, D_MODEL), lambda i: (i, 0)),
        compiler_params=_params(("parallel",)),
        name="matmul_residual",
    )(a_bf16, w_bf16, res)


def _final_norm_kernel(x_ref, g_ref, o_ref):
    o_ref[...] = _rmsnorm(x_ref[...], g_ref[...])


def _final_norm(x, g):
    return pl.pallas_call(
        _final_norm_kernel,
        out_shape=jax.ShapeDtypeStruct((N_TOK, D_MODEL), F32),
        grid=(N_ROW_TILES,),
        in_specs=[
            pl.BlockSpec((ROW_TILE, D_MODEL), lambda i: (i, 0)),
            pl.BlockSpec((1, D_MODEL), lambda i: (0, 0)),
        ],
        out_specs=pl.BlockSpec((ROW_TILE, D_MODEL), lambda i: (i, 0)),
        compiler_params=_params(("parallel",)),
        name="final_norm",
    )(x, g.reshape(1, D_MODEL))


def _t5_bucket_np(dist):
    n = np.maximum(dist, 0)
    max_exact = NUM_BUCKETS // 2
    nf = np.maximum(n, 1).astype(np.float32)
    large = max_exact + (np.log(nf / np.float32(max_exact)) / np.float32(math.log(MAX_DISTANCE / max_exact))
                         * np.float32(NUM_BUCKETS - max_exact)).astype(np.int32)
    large = np.minimum(large, NUM_BUCKETS - 1)
    return np.where(n < max_exact, n, large).astype(np.int32)


def _bias_table_kernel(rel_ref, bucket_ref, o_ref):
    bucket = bucket_ref[...]
    for h in range(N_HEADS):
        acc = jnp.full(bucket.shape, NEG, F32)
        for b in range(NUM_BUCKETS):
            acc = jnp.where(bucket == b, rel_ref[b, h], acc)
        o_ref[h] = acc


def _bias_table(rel_bias, bucket_np):
    rows, cols = bucket_np.shape
    return pl.pallas_call(
        _bias_table_kernel,
        out_shape=jax.ShapeDtypeStruct((N_HEADS, rows, cols), F32),
        in_specs=[
            pl.BlockSpec(memory_space=pltpu.SMEM),
            pl.BlockSpec((rows, cols), lambda: (0, 0)),
        ],
        out_specs=pl.BlockSpec((N_HEADS, rows, cols), lambda: (0, 0, 0)),
        name="bias_table",
    )(rel_bias, jnp.asarray(bucket_np))


def _prompt_buckets():
    r = np.arange(BLOCK)[:, None]
    c = np.arange(2 * BLOCK)[None, :]
    dist = r - c + BLOCK
    valid = (dist >= 0) & (dist < WINDOW)
    rest = np.where(valid, _t5_bucket_np(dist), -1).astype(np.int32)
    first = np.where(c >= BLOCK, rest, -1).astype(np.int32)
    return np.concatenate([first, rest], axis=0)


def _sample_buckets():
    j = np.arange(DEC_SEQ)[:, None]
    c = np.arange(KEYS_SAMPLE_PAD)[None, :]
    dist = j + WINDOW - c
    valid = (dist >= 0) & (dist < WINDOW) & (c < KEYS_SAMPLE)
    return np.where(valid, _t5_bucket_np(dist), -1).astype(np.int32)


def _softmax_sink_pv(s_list, v_list, sink):
    m = sink
    for s in s_list:
        m = jnp.maximum(m, jnp.max(s, axis=-1, keepdims=True))
    denom = jnp.exp(sink - m)
    acc = None
    for s, v in zip(s_list, v_list):
        p = jnp.exp(s - m)
        denom = denom + jnp.sum(p, axis=-1, keepdims=True)
        pv = jnp.dot(p.astype(BF16), v, preferred_element_type=F32)
        acc = pv if acc is None else acc + pv
    return acc / denom


def _attn_prompt_kernel(sink_ref, q_ref, kp_ref, kc_ref, vp_ref, vc_ref, bias_ref, o_ref):
    kp = kp_ref[...].astype(BF16)
    kc = kc_ref[...].astype(BF16)
    vp = vp_ref[...].astype(BF16)
    vc = vc_ref[...].astype(BF16)
    outs = []
    for g in range(N_KV_HEADS):
        ksl = slice(g * HEAD_DIM, (g + 1) * HEAD_DIM)
        qg = jnp.concatenate(
            [q_ref[:, (g * GROUP + hh) * HEAD_DIM:(g * GROUP + hh + 1) * HEAD_DIM] for hh in range(GROUP)],
            axis=0).astype(BF16)
        bias = jnp.concatenate([bias_ref[g * GROUP + hh] for hh in range(GROUP)], axis=0)
        sink = jnp.concatenate(
            [jnp.full((BLOCK, 1), sink_ref[g * GROUP + hh], F32) for hh in range(GROUP)], axis=0)
        dn = (((1,), (1,)), ((), ()))
        s_prev = lax.dot_general(qg, kp[:, ksl], dn, preferred_element_type=F32) * ATTN_SCALE + bias[:, :BLOCK]
        s_cur = lax.dot_general(qg, kc[:, ksl], dn, preferred_element_type=F32) * ATTN_SCALE + bias[:, BLOCK:]
        og = _softmax_sink_pv([s_prev, s_cur], [vp[:, ksl], vc[:, ksl]], sink)
        for hh in range(GROUP):
            outs.append(og[hh * BLOCK:(hh + 1) * BLOCK])
    o_ref[...] = jnp.concatenate(outs, axis=1).astype(o_ref.dtype)


def _attn_prompt(qkv, sinks, bias_tab):
    nb = SEQ // BLOCK
    kcol = N_HEADS * HEAD_DIM // KV_DIM
    row = lambda b, n: b * nb + n
    prev = lambda b, n: b * nb + jnp.maximum(n - 1, 0)
    return pl.pallas_call(
        _attn_prompt_kernel,
        out_shape=jax.ShapeDtypeStruct((N_PROMPT, D_MODEL), BF16),
        grid=(BATCH, nb),
        in_specs=[
            pl.BlockSpec(memory_space=pltpu.SMEM),
            pl.BlockSpec((BLOCK, N_HEADS * HEAD_DIM), lambda b, n: (row(b, n), 0)),
            pl.BlockSpec((BLOCK, KV_DIM), lambda b, n: (prev(b, n), kcol)),
            pl.BlockSpec((BLOCK, KV_DIM), lambda b, n: (row(b, n), kcol)),
            pl.BlockSpec((BLOCK, KV_DIM), lambda b, n: (prev(b, n), kcol + 1)),
            pl.BlockSpec((BLOCK, KV_DIM), lambda b, n: (row(b, n), kcol + 1)),
            pl.BlockSpec((N_HEADS, BLOCK, 2 * BLOCK), lambda b, n: (0, jnp.minimum(n, 1), 0)),
        ],
        out_specs=pl.BlockSpec((BLOCK, D_MODEL), lambda b, n: (row(b, n), 0)),
        compiler_params=_params(("parallel", "parallel")),
        name="attn_prompt",
    )(sinks, qkv, qkv, qkv, qkv, qkv, bias_tab)


def _attn_sample_kernel(q_ref, k_ref, v_ref, bias_ref, sink_ref, o_ref):
    nb = SAMPLE_SEQ_BLOCK * N_KV_HEADS
    rows = DEC_SEQ * GROUP
    q = q_ref[...].reshape(nb, rows, HEAD_DIM)
    k = k_ref[...].reshape(nb, KEYS_SAMPLE_PAD, HEAD_DIM)
    v = v_ref[...].reshape(nb, KEYS_SAMPLE_PAD, HEAD_DIM)
    s = jnp.einsum("nqd,nkd->nqk", q, k, preferred_element_type=F32) * ATTN_SCALE
    s = s.reshape(SAMPLE_SEQ_BLOCK, N_KV_HEADS, rows, KEYS_SAMPLE_PAD) + bias_ref[...][None]
    sink = sink_ref[...][None]
    m = jnp.maximum(jnp.max(s, axis=-1, keepdims=True), sink)
    p = jnp.exp(s - m)
    denom = jnp.sum(p, axis=-1, keepdims=True) + jnp.exp(sink - m)
    pv = jnp.einsum("nqk,nkd->nqd", p.reshape(nb, rows, KEYS_SAMPLE_PAD).astype(BF16), v,
                    preferred_element_type=F32)
    o_ref[...] = pv.reshape(SAMPLE_SEQ_BLOCK, N_KV_HEADS, rows, HEAD_DIM) / denom


def _attn_sample(qg, kc, vc, bias_s, sink_s):
    rows = DEC_SEQ * GROUP
    blk = lambda shape: pl.BlockSpec((SAMPLE_SEQ_BLOCK,) + shape, lambda i: (i, 0, 0, 0))
    return pl.pallas_call(
        _attn_sample_kernel,
        out_shape=jax.ShapeDtypeStruct((DEC_BATCH, N_KV_HEADS, rows, HEAD_DIM), F32),
        grid=(DEC_BATCH // SAMPLE_SEQ_BLOCK,),
        in_specs=[
            blk((N_KV_HEADS, rows, HEAD_DIM)),
            blk((N_KV_HEADS, KEYS_SAMPLE_PAD, HEAD_DIM)),
            blk((N_KV_HEADS, KEYS_SAMPLE_PAD, HEAD_DIM)),
            pl.BlockSpec((N_KV_HEADS, rows, KEYS_SAMPLE_PAD), lambda i: (0, 0, 0)),
            pl.BlockSpec((N_KV_HEADS, rows, 1), lambda i: (0, 0, 0)),
        ],
        out_specs=blk((N_KV_HEADS, rows, HEAD_DIM)),
        compiler_params=_params(("parallel",)),
        name="attn_sample",
    )(qg, kc, vc, bias_s, sink_s)


def _conv_prompt_kernel(bch_ref, w_ref, wout_ref, r_ref, o_ref, ulast_ref, ubuf):
    i = pl.program_id(0)

    @pl.when(i % TILES_PER_SEQ == 0)
    def _():
        ubuf[0:SUBLANES, :] = jnp.zeros((SUBLANES, D_CONV), F32)

    u = bch_ref[:, D_CONV:2 * D_CONV] * bch_ref[:, 2 * D_CONV:]
    ubuf[SUBLANES:, :] = u
    y = (w_ref[0:1, :] * ubuf[SUBLANES - 2:SUBLANES - 2 + ROW_TILE, :]
         + w_ref[1:2, :] * ubuf[SUBLANES - 1:SUBLANES - 1 + ROW_TILE, :]
         + w_ref[2:3, :] * u)
    gated = (bch_ref[:, :D_CONV] * y).astype(BF16)
    o_ref[...] = r_ref[...] + jnp.dot(gated, wout_ref[...], preferred_element_type=F32)
    tail = u[ROW_TILE - SUBLANES:, :]
    ulast_ref[...] = tail
    ubuf[0:SUBLANES, :] = tail


def _conv_prompt(bch, conv_w, wout_bf16, res):
    return pl.pallas_call(
        _conv_prompt_kernel,
        out_shape=(jax.ShapeDtypeStruct((N_PROMPT, D_MODEL), F32),
                   jax.ShapeDtypeStruct((PROMPT_ROW_TILES * SUBLANES, D_CONV), F32)),
        grid=(PROMPT_ROW_TILES,),
        in_specs=[
            pl.BlockSpec((ROW_TILE, 3 * D_CONV), lambda i: (i, 0)),
            pl.BlockSpec((CONV_W, D_CONV), lambda i: (0, 0)),
            pl.BlockSpec((D_CONV, D_MODEL), lambda i: (0, 0)),
            pl.BlockSpec((ROW_TILE, D_MODEL), lambda i: (i, 0)),
        ],
        out_specs=(pl.BlockSpec((ROW_TILE, D_MODEL), lambda i: (i, 0)),
                   pl.BlockSpec((SUBLANES, D_CONV), lambda i: (i, 0))),
        scratch_shapes=[pltpu.VMEM((ROW_TILE + SUBLANES, D_CONV), F32)],
        compiler_params=_params(("arbitrary",)),
        name="conv_prompt",
    )(bch, conv_w, wout_bf16, res)


def _conv_sample_kernel(bch_ref, st_ref, w_ref, wout_ref, r_ref, o_ref, u_ref):
    u = bch_ref[:, D_CONV:2 * D_CONV] * bch_ref[:, 2 * D_CONV:]
    u_ref[...] = u
    up = [st_ref[0:DEC_BATCH, :], st_ref[DEC_BATCH:, :]] + [u[t * DEC_BATCH:(t + 1) * DEC_BATCH] for t in range(DEC_SEQ)]
    y = jnp.concatenate(
        [w_ref[0:1, :] * up[t] + w_ref[1:2, :] * up[t + 1] + w_ref[2:3, :] * up[t + 2] for t in range(DEC_SEQ)], axis=0)
    gated = (bch_ref[:, :D_CONV] * y).astype(BF16)
    o_ref[...] = r_ref[...] + jnp.dot(gated, wout_ref[...], preferred_element_type=F32)


def _conv_sample(bch_s, state_tm, conv_w, wout_bf16, res_s):
    full = lambda shape: pl.BlockSpec(shape, lambda: (0,) * len(shape))
    return pl.pallas_call(
        _conv_sample_kernel,
        out_shape=(jax.ShapeDtypeStruct((N_SAMPLE, D_MODEL), F32),
                   jax.ShapeDtypeStruct((N_SAMPLE, D_CONV), F32)),
        in_specs=[full((N_SAMPLE, 3 * D_CONV)), full(((CONV_W - 1) * DEC_BATCH, D_CONV)),
                  full((CONV_W, D_CONV)), full((D_CONV, D_MODEL)), full((N_SAMPLE, D_MODEL))],
        out_specs=(full((N_SAMPLE, D_MODEL)), full((N_SAMPLE, D_CONV))),
        compiler_params=pltpu.CompilerParams(vmem_limit_bytes=VMEM_LIMIT_BYTES),
        name="conv_sample",
    )(bch_s, state_tm, conv_w, wout_bf16, res_s)


def _extract_max(vals, idx):
    m = jnp.max(vals, axis=0, keepdims=True)
    first = jnp.min(jnp.where(vals == m, idx, jnp.float32(1e9)), axis=0, keepdims=True)
    hit = idx == first
    return m, hit, jnp.where(hit, -jnp.inf, vals)


def _bf16_high_bits(v):
    bits = pltpu.bitcast(v, jnp.uint32)
    return (bits + jnp.uint32(0x7FFF) + ((bits >> 16) & jnp.uint32(1))) & jnp.uint32(0xFFFF0000)


def _bf16_pair_words(v):
    high = _bf16_high_bits(v)
    return high | (high >> 16)


def _bf16_split_words(v):
    half = v.shape[0] // 2
    return _bf16_high_bits(v[half:]) | (_bf16_high_bits(v[:half]) >> 16)


def _sort16_comparators():
    pairs = []

    def merge(lo, n, step):
        double = step * 2
        if double < n:
            merge(lo, n, double)
            merge(lo + step, n, double)
            for i in range(lo + step, lo + n - step, double):
                pairs.append((i, i + step))
        else:
            pairs.append((lo, lo + step))

    def sort(lo, n):
        if n > 1:
            half = n // 2
            sort(lo, half)
            sort(lo + half, half)
            merge(lo, n, 1)

    sort(0, PEER_TOPK)
    return pairs


_SORT16 = _sort16_comparators()


def _compare_exchange(x, i, j):
    x[i], x[j] = jnp.maximum(x[i], x[j]), jnp.minimum(x[i], x[j])


def _sorted_top16(x):
    x = list(x)
    for i, j in _SORT16:
        _compare_exchange(x, i, j)
    for shift in (4, 2, 1):
        y = [pltpu.roll(v, shift, axis=0) for v in x]
        x = [jnp.maximum(x[i], y[PEER_TOPK - 1 - i]) for i in range(PEER_TOPK)]
        d = PEER_TOPK // 2
        while d:
            for i in range(PEER_TOPK):
                if not i & d:
                    _compare_exchange(x, i, i + d)
            d //= 2
    return x


def _rank_among(top, v):
    rank = jnp.zeros(v.shape, F32)
    for r in range(PEER_TOPK):
        rank = jnp.where(top[r] > v, jnp.float32(r + 1), rank)
    return rank


_CAND_COLS = [PEER_TOPK // (r + 1) for r in range(PEER_TOPK)]


def _peer_route_kernel(x_ref, g_ref, wq_ref, sk_ref, cidx_ref, cvalid_ref,
                       xnt_ref, nsel_ref, a_ref, rb_ref, b_ref,
                       q_scr, s_scr, rank_scr, sorted_scr):
    xn = _rmsnorm(x_ref[...], g_ref[...])
    xnt = xn.T.astype(BF16)
    xnt_ref[...] = xnt
    q_scr[...] = jnp.dot(wq_ref[...], xnt, preferred_element_type=F32).astype(BF16)
    n_lane_tiles = ROUTE_TILE // LANES
    n_groups = N_KEYS // SUBLANES

    sub = lax.broadcasted_iota(jnp.int32, (SUBLANES, LANES), 0)

    def fast_head(h, tied):
        scores = []
        for p in range(2):
            hp = 2 * h + p
            qs = q_scr[pl.ds(pl.multiple_of(hp * D_KEY_HALF, D_KEY_HALF), D_KEY_HALF), :]
            s_all = jnp.dot(sk_ref[hp], qs, preferred_element_type=F32)
            s_scr[hp] = s_all
            scores.append(s_all)
        for lt in range(n_lane_tiles):
            lanes = slice(lt * LANES, (lt + 1) * LANES)
            tops, ranks, vals = [], [], []
            for p in range(2):
                v = [scores[p][k * SUBLANES:(k + 1) * SUBLANES, lanes] for k in range(n_groups)]
                top = _sorted_top16(v)
                rank = [_rank_among(top, vk) for vk in v]
                members = sum(jnp.where(rk < PEER_TOPK, 1.0, 0.0) for rk in rank)
                members = jnp.sum(members, axis=0, keepdims=True)
                tied = jnp.maximum(tied, jnp.where(members != PEER_TOPK, 1.0, 0.0))
                for r in range(PEER_TOPK - 1):
                    tied = jnp.maximum(tied, jnp.where(top[r] == top[r + 1], 1.0, 0.0))
                tops.append(top)
                ranks.append(rank)
                vals.append(v)
            a_top, b_top = tops
            b_col = [jnp.where(sub == c, b_top[c], 0.0) for c in range(PEER_TOPK)]
            b_lo = sum(b_col[:SUBLANES])
            b_hi = sum(b_col[SUBLANES:])
            a_hi = sum(jnp.where(sub == r, a_top[SUBLANES + r], 0.0) for r in range(SUBLANES))
            cands = [a_top[0] + b_lo, a_top[0] + b_hi]
            for r in range(1, SUBLANES):
                cands.append(jnp.where(sub < _CAND_COLS[r], a_top[r] + b_lo, -jnp.inf))
            cands.append(a_hi + b_top[0])
            minus_inf = jnp.full((SUBLANES, LANES), -jnp.inf, F32)
            sums = _sorted_top16(cands + [minus_inf] * (PEER_TOPK - len(cands)))
            z = sum(jnp.exp(sums[r] - sums[0]) for r in range(PEER_TOPK))
            picked = [jnp.where(c >= sums[PEER_TOPK - 1], 1.0, 0.0) for c in cands]
            counts = [jnp.sum(picked[0] + picked[1], axis=0, keepdims=True)]
            counts += [jnp.sum(picked[r + 1], axis=0, keepdims=True) for r in range(1, SUBLANES)]
            counts += [picked[SUBLANES + 1][r:r + 1, :] for r in range(SUBLANES)]
            tied = jnp.maximum(tied, jnp.where(sum(counts) != PEER_TOPK, 1.0, 0.0))
            nsel = []
            for rk in ranks[0]:
                n = jnp.zeros((SUBLANES, LANES), F32)
                for r in range(PEER_TOPK):
                    n = jnp.where(rk == r, counts[r], n)
                nsel.append(n)
            nsel_ref[h, :, lanes] = _bf16_pair_words(jnp.concatenate(nsel, axis=0))
            a_ref[h, :, lanes] = _bf16_pair_words(jnp.exp(jnp.concatenate(vals[0], axis=0) - a_top[0][0:1, :]))
            rb_ref[h, :, lanes] = _bf16_split_words(jnp.concatenate(ranks[1], axis=0))
            b_ref[h, :, lanes] = _bf16_split_words(
                jnp.exp(jnp.concatenate(vals[1], axis=0) - b_top[0][0:1, :]) / z[0:1, :])
        return tied

    tied = lax.fori_loop(0, PEER_HEADS, fast_head, jnp.zeros((SUBLANES, LANES), F32))

    @pl.when(jnp.max(tied) > 0.0)
    def _():
        _route_exact(cidx_ref, cvalid_ref, nsel_ref, a_ref, rb_ref, b_ref, s_scr, rank_scr, sorted_scr)


def _route_exact(cidx_ref, cvalid_ref, nsel_ref, a_ref, rb_ref, b_ref, s_scr, rank_scr, sorted_scr):
    key_idx = lax.broadcasted_iota(jnp.int32, (N_KEYS, LANES), 0).astype(F32)
    n_lane_tiles = ROUTE_TILE // LANES

    def stage1(hp, carry):
        for lt in range(n_lane_tiles):
            lanes = slice(lt * LANES, (lt + 1) * LANES)
            vals = s_scr[hp, :, lanes]
            rank = jnp.full((N_KEYS, LANES), NOT_SELECTED_RANK, F32)
            for it in range(PEER_TOPK):
                m, hit, vals = _extract_max(vals, key_idx)
                rank = jnp.where(hit, jnp.float32(it), rank)
                sorted_scr[hp, it:it + 1, lanes] = m
            rank_scr[hp, :, lanes] = rank
        return carry

    lax.fori_loop(0, 2 * PEER_HEADS, stage1, 0)

    cidx = cidx_ref[...]
    cvalid = cvalid_ref[...] > 0.5

    def stage2(h, carry):
        for lt in range(n_lane_tiles):
            lanes = slice(lt * LANES, (lt + 1) * LANES)
            a_s = sorted_scr[2 * h, :, lanes]
            b_s = sorted_scr[2 * h + 1, :, lanes]
            main = [a_s[r:r + 1, :] + b_s for r in range(CAND_MAIN_ROWS)]
            tail = a_s[CAND_MAIN_ROWS:, :] + b_s[0:1, :]
            cand = jnp.where(cvalid, jnp.concatenate(main + [tail], axis=0), -jnp.inf)
            sel = jnp.zeros((CAND_ROWS, LANES), F32)
            top = a_s[0:1, :] + b_s[0:1, :]
            z = jnp.zeros((1, LANES), F32)
            for it in range(PEER_TOPK):
                m, hit, cand = _extract_max(cand, cidx)
                sel = jnp.where(hit, 1.0, sel)
                z = z + jnp.exp(m - top)
            rank_a = rank_scr[2 * h, :, lanes]
            nsel = jnp.zeros((N_KEYS, LANES), F32)
            for r in range(PEER_TOPK):
                if r < CAND_MAIN_ROWS:
                    n_r = jnp.sum(sel[r * PEER_TOPK:(r + 1) * PEER_TOPK, :], axis=0, keepdims=True)
                else:
                    row = CAND_MAIN_ROWS * PEER_TOPK + r - CAND_MAIN_ROWS
                    n_r = sel[row:row + 1, :]
                nsel = jnp.where(rank_a == jnp.float32(r), n_r, nsel)
            nsel_ref[h, :, lanes] = _bf16_pair_words(nsel)
            a_ref[h, :, lanes] = _bf16_pair_words(jnp.exp(s_scr[2 * h, :, lanes] - a_s[0:1, :]))
            rb_ref[h, :, lanes] = _bf16_split_words(rank_scr[2 * h + 1, :, lanes])
            b_ref[h, :, lanes] = _bf16_split_words(jnp.exp(s_scr[2 * h + 1, :, lanes] - b_s[0:1, :]) / z)
        return carry

    lax.fori_loop(0, PEER_HEADS, stage2, 0)


def _candidate_tables():
    rows = np.arange(CAND_ROWS)
    main = rows < CAND_MAIN_ROWS * PEER_TOPK
    r = np.where(main, rows // PEER_TOPK, CAND_MAIN_ROWS + rows - CAND_MAIN_ROWS * PEER_TOPK)
    c = np.where(main, rows % PEER_TOPK, 0)
    idx = (r * PEER_TOPK + c).astype(np.float32)
    valid = ((r + 1) * (c + 1) <= PEER_TOPK).astype(np.float32)
    tile = lambda v: np.ascontiguousarray(np.broadcast_to(v[:, None], (CAND_ROWS, LANES)))
    return tile(idx), tile(valid)


def _peer_route(h_all, g, wq_t_bf16, sub_keys_bf16):
    cidx, cvalid = _candidate_tables()
    n_tiles = N_TOK // ROUTE_TILE
    words = jax.ShapeDtypeStruct((PEER_HEADS, N_KEYS, N_TOK), jnp.uint32)
    halves = jax.ShapeDtypeStruct((PEER_HEADS, N_KEYS // 2, N_TOK), jnp.uint32)
    head_spec = pl.BlockSpec((PEER_HEADS, N_KEYS, ROUTE_TILE), lambda i: (0, 0, i))
    half_spec = pl.BlockSpec((PEER_HEADS, N_KEYS // 2, ROUTE_TILE), lambda i: (0, 0, i))
    return pl.pallas_call(
        _peer_route_kernel,
        out_shape=(jax.ShapeDtypeStruct((D_MODEL, N_TOK), BF16), words, words, halves, halves),
        grid=(n_tiles,),
        in_specs=[
            pl.BlockSpec((ROUTE_TILE, D_MODEL), lambda i: (i, 0)),
            pl.BlockSpec((1, D_MODEL), lambda i: (0, 0)),
            pl.BlockSpec((2 * PEER_HEADS * D_KEY_HALF, D_MODEL), lambda i: (0, 0)),
            pl.BlockSpec((2 * PEER_HEADS, N_KEYS, D_KEY_HALF), lambda i: (0, 0, 0)),
            pl.BlockSpec((CAND_ROWS, LANES), lambda i: (0, 0)),
            pl.BlockSpec((CAND_ROWS, LANES), lambda i: (0, 0)),
        ],
        out_specs=(pl.BlockSpec((D_MODEL, ROUTE_TILE), lambda i: (0, i)),
                   head_spec, head_spec, half_spec, half_spec),
        scratch_shapes=[
            pltpu.VMEM((2 * PEER_HEADS * D_KEY_HALF, ROUTE_TILE), BF16),
            pltpu.VMEM((2 * PEER_HEADS, N_KEYS, ROUTE_TILE), F32),
            pltpu.VMEM((2 * PEER_HEADS, N_KEYS, ROUTE_TILE), F32),
            pltpu.VMEM((2 * PEER_HEADS, PEER_TOPK, ROUTE_TILE), F32),
        ],
        compiler_params=_params(("parallel",)),
        name="peer_route",
    )(h_all, g.reshape(1, D_MODEL), wq_t_bf16, sub_keys_bf16, jnp.asarray(cidx), jnp.asarray(cvalid))


def _gelu(x):
    return 0.5 * x * (1.0 + lax.erf(x * np.float32(1.0 / math.sqrt(2.0))))


def _packed_row_broadcast(words, row):
    return pltpu.bitcast(jnp.broadcast_to(words[row:row + 1, :], (SUBLANES, LANES)), BF16)


def _peer_dense_kernel(xnt_ref, u_ref, vt_ref, nsel_ref, a_ref, rb_ref, b_ref, res_ref, o_ref,
                       acc_scr, h_scr, w_scr, z_new_scr, z_prev_scr):
    e = pl.program_id(1)
    n_blocks = pl.num_programs(1) - 1

    @pl.when(e == 0)
    def _():
        acc_scr[...] = jnp.zeros_like(acc_scr)
        z_new_scr[...] = jnp.zeros_like(z_new_scr)

    @pl.when(e < n_blocks)
    def _():
        z_prev_scr[...] = z_new_scr[...]
        first_keys = pl.ds(pl.multiple_of(e * KEYS_PER_BLOCK, SUBLANES), KEYS_PER_BLOCK)
        n_chunks = N_KEYS // PACKED_ROWS
        zero = jnp.zeros((PACKED_ROWS, LANES), BF16)
        for i0 in range(0, KEYS_PER_BLOCK, KEYS_PER_GATE_GROUP):
            group = range(i0, i0 + KEYS_PER_GATE_GROUP)
            for lt, k0 in itertools.product(range(DENSE_TILE // LANES), range(0, n_chunks, CHUNKS_PER_GATE_GROUP)):
                lanes = slice(lt * LANES, (lt + 1) * LANES)
                chunks = range(k0, k0 + CHUNKS_PER_GATE_GROUP)
                w = {(ii, k): zero for ii in group for k in chunks}
                for h in range(PEER_HEADS):
                    n_words = nsel_ref[h, first_keys, lanes]
                    a_words = a_ref[h, first_keys, lanes]
                    n_b = {ii: _packed_row_broadcast(n_words, ii) for ii in group}
                    a_b = {ii: _packed_row_broadcast(a_words, ii) for ii in group}
                    for k in chunks:
                        second = slice(k * SUBLANES, (k + 1) * SUBLANES)
                        rank_b = pltpu.bitcast(rb_ref[h, second, lanes], BF16)
                        gate_b = pltpu.bitcast(b_ref[h, second, lanes], BF16)
                        for ii in group:
                            w[ii, k] = w[ii, k] + a_b[ii] * jnp.where(rank_b < n_b[ii], gate_b, zero)
                for ii in group:
                    for k in chunks:
                        words = slice((ii * n_chunks + k) * SUBLANES, (ii * n_chunks + k + 1) * SUBLANES)
                        w_scr[words, lanes] = pltpu.bitcast(w[ii, k], jnp.uint32)
        h_scr[...] = jnp.dot(u_ref[...], xnt_ref[...], preferred_element_type=F32)
        acc_scr[...] += jnp.dot(vt_ref[...], z_prev_scr[...], preferred_element_type=F32)
        for ii, m, lt in itertools.product(range(KEYS_PER_BLOCK), range(n_chunks // 2), range(DENSE_TILE // LANES)):
            lanes = slice(lt * LANES, (lt + 1) * LANES)
            first_word = (ii * n_chunks + 2 * m) * SUBLANES
            words = w_scr[first_word:first_word + PACKED_ROWS, lanes]
            gates = (pltpu.bitcast(words << 16, F32), pltpu.bitcast(words & jnp.uint32(0xFFFF0000), F32))
            for half, gate in enumerate(gates):
                row0 = ii * N_KEYS + half * (N_KEYS // 2) + m * PACKED_ROWS
                rows = slice(row0, row0 + PACKED_ROWS)
                z_new_scr[rows, lanes] = (_gelu(h_scr[rows, lanes]) * gate).astype(BF16)

    @pl.when(e == n_blocks)
    def _():
        z_prev_scr[...] = z_new_scr[...]
        acc = acc_scr[...] + jnp.dot(vt_ref[...], z_prev_scr[...], preferred_element_type=F32)
        o_ref[...] = res_ref[...] + acc.T


def _peer_dense(xnt, u_bf16, vt_bf16, nsel, a, rb, b, res):
    n_blocks = N_EXPERTS // EXPERT_BLOCK
    head_spec = pl.BlockSpec((PEER_HEADS, N_KEYS, DENSE_TILE), lambda t, e: (0, 0, t))
    half_spec = pl.BlockSpec((PEER_HEADS, N_KEYS // 2, DENSE_TILE), lambda t, e: (0, 0, t))
    return pl.pallas_call(
        _peer_dense_kernel,
        out_shape=jax.ShapeDtypeStruct((N_TOK, D_MODEL), F32),
        grid=(N_TOK // DENSE_TILE, n_blocks + 1),
        in_specs=[
            pl.BlockSpec((D_MODEL, DENSE_TILE), lambda t, e: (0, t)),
            pl.BlockSpec((EXPERT_BLOCK, D_MODEL), lambda t, e: (jnp.minimum(e, n_blocks - 1), 0)),
            pl.BlockSpec((D_MODEL, EXPERT_BLOCK), lambda t, e: (0, jnp.maximum(e - 1, 0))),
            head_spec, head_spec, half_spec, half_spec,
            pl.BlockSpec((DENSE_TILE, D_MODEL), lambda t, e: (t, 0)),
        ],
        out_specs=pl.BlockSpec((DENSE_TILE, D_MODEL), lambda t, e: (t, 0)),
        scratch_shapes=[
            pltpu.VMEM((D_MODEL, DENSE_TILE), F32),
            pltpu.VMEM((EXPERT_BLOCK, DENSE_TILE), F32),
            pltpu.VMEM((EXPERT_BLOCK // 2, DENSE_TILE), jnp.uint32),
            pltpu.VMEM((EXPERT_BLOCK, DENSE_TILE), BF16),
            pltpu.VMEM((EXPERT_BLOCK, DENSE_TILE), BF16),
        ],
        compiler_params=pltpu.CompilerParams(
            dimension_semantics=("parallel", "arbitrary"), vmem_limit_bytes=VMEM_LIMIT_BYTES),
        name="peer_dense",
    )(xnt, u_bf16, vt_bf16, nsel, a, rb, b, res)


def _peer(h_all, g, w_q, sub_keys, u_tab, v_tab):
    wq_t = w_q.T.astype(BF16)
    sk = sub_keys.reshape(2 * PEER_HEADS, N_KEYS, D_KEY_HALF).astype(BF16)
    xnt, nsel, a, rb, b = _peer_route(h_all, g, wq_t, sk)
    return _peer_dense(xnt, u_tab.astype(BF16), v_tab.T.astype(BF16), nsel, a, rb, b, h_all)


def _attention_layer(h_all, g, w_qkv, sinks, w_o, cache_k, cache_v, bias_p, bias_s):
    qkv = _norm_matmul(h_all, g, w_qkv.astype(BF16))
    o_p = _attn_prompt(qkv, sinks, bias_p)

    hq = N_HEADS * HEAD_DIM
    qkv_s = qkv[N_PROMPT:].reshape(DEC_SEQ, DEC_BATCH, QKV_DIM)
    q_s = qkv_s[..., :hq].reshape(DEC_SEQ, DEC_BATCH, N_KV_HEADS, GROUP, HEAD_DIM)
    qg = q_s.transpose(1, 2, 0, 3, 4).reshape(DEC_BATCH, N_KV_HEADS, DEC_SEQ * GROUP, HEAD_DIM)
    k_new = qkv_s[..., hq:hq + KV_DIM].reshape(DEC_SEQ, DEC_BATCH, N_KV_HEADS, HEAD_DIM).transpose(1, 0, 2, 3)
    v_new = qkv_s[..., hq + KV_DIM:].reshape(DEC_SEQ, DEC_BATCH, N_KV_HEADS, HEAD_DIM).transpose(1, 0, 2, 3)
    kc = jnp.concatenate([cache_k, k_new], axis=1)
    vc = jnp.concatenate([cache_v, v_new], axis=1)
    pad = ((0, 0), (0, 0), (0, KEYS_SAMPLE_PAD - KEYS_SAMPLE), (0, 0))
    kc_g = jnp.pad(kc.transpose(0, 2, 1, 3), pad).astype(BF16)
    vc_g = jnp.pad(vc.transpose(0, 2, 1, 3), pad).astype(BF16)
    sink_s = jnp.broadcast_to(sinks.reshape(N_KV_HEADS, 1, GROUP), (N_KV_HEADS, DEC_SEQ, GROUP))
    sink_s = sink_s.reshape(N_KV_HEADS, DEC_SEQ * GROUP, 1)
    og = _attn_sample(qg.astype(BF16), kc_g, vc_g, bias_s, sink_s)
    o_s = og.reshape(DEC_BATCH, N_KV_HEADS, DEC_SEQ, GROUP, HEAD_DIM).transpose(2, 0, 1, 3, 4)
    o_s = o_s.reshape(N_SAMPLE, D_MODEL).astype(BF16)

    h_new = _matmul_residual(jnp.concatenate([o_p, o_s], axis=0), w_o.astype(BF16), h_all)

    k_p = qkv[:N_PROMPT, hq:hq + KV_DIM].reshape(BATCH, SEQ, N_KV_HEADS, HEAD_DIM)[:, SEQ - WINDOW:]
    v_p = qkv[:N_PROMPT, hq + KV_DIM:].reshape(BATCH, SEQ, N_KV_HEADS, HEAD_DIM)[:, SEQ - WINDOW:]
    return h_new, k_p, v_p, kc[:, -WINDOW:], vc[:, -WINDOW:]


def _conv_layer(h_all, g, w_in, conv_w, w_out, state):
    bch = _norm_matmul(h_all, g, w_in.astype(BF16))
    wout = w_out.astype(BF16)
    h_p, ulast = _conv_prompt(bch, conv_w, wout, h_all)
    state_tm = state.transpose(1, 0, 2).reshape((CONV_W - 1) * DEC_BATCH, D_CONV)
    h_s, u_s = _conv_sample(bch[N_PROMPT:], state_tm, conv_w, wout, h_all[N_PROMPT:])
    ulast = ulast.reshape(BATCH, TILES_PER_SEQ, SUBLANES, D_CONV)
    conv_p = ulast[:, -1, SUBLANES - (CONV_W - 1):, :]
    conv_s = u_s.reshape(DEC_SEQ, DEC_BATCH, D_CONV)[DEC_SEQ - (CONV_W - 1):].transpose(1, 0, 2)
    return jnp.concatenate([h_p, h_s], axis=0), conv_p, conv_s


def _bias_tables(rel_bias):
    bias_p = _bias_table(rel_bias, _prompt_buckets())
    sb = _bias_table(rel_bias, np.pad(_sample_buckets(), ((0, SUBLANES - DEC_SEQ), (0, 0)), constant_values=-1))
    sb = sb[:, :DEC_SEQ, :]
    bias_s = sb.reshape(N_KV_HEADS, GROUP, DEC_SEQ, KEYS_SAMPLE_PAD).transpose(0, 2, 1, 3)
    return bias_p, bias_s.reshape(N_KV_HEADS, DEC_SEQ * GROUP, KEYS_SAMPLE_PAD)


def kernel(x_prompt, x_sample, cache_k, cache_v, state_conv, norm_mix_g, norm_ffn_g, norm_final_g, rel_bias, attn_w_qkv, attn_sinks, attn_w_o, conv_w_in, conv_w, conv_w_out, peer_w_q, peer_sub_keys, peer_u, peer_v):
    h_all = jnp.concatenate(
        [x_prompt.reshape(N_PROMPT, D_MODEL), x_sample.transpose(1, 0, 2).reshape(N_SAMPLE, D_MODEL)], axis=0)
    bias_p, bias_s = _bias_tables(rel_bias)
    kp_l, vp_l, cp_l, ks_l, vs_l, cs_l = [], [], [], [], [], []
    for i in range(DEPTH):
        j = i // 2
        if i % 2 == 0:
            h_all, kp, vp, kn, vn = _attention_layer(
                h_all, norm_mix_g[i], attn_w_qkv[j], attn_sinks[j], attn_w_o[j], cache_k[j], cache_v[j],
                bias_p, bias_s)
            kp_l.append(kp); vp_l.append(vp); ks_l.append(kn); vs_l.append(vn)
        else:
            h_all, cp, cn = _conv_layer(h_all, norm_mix_g[i], conv_w_in[j], conv_w[j], conv_w_out[j], state_conv[j])
            cp_l.append(cp); cs_l.append(cn)
        h_all = _peer(h_all, norm_ffn_g[i], peer_w_q[i], peer_sub_keys[i], peer_u[i], peer_v[i])
    y = _final_norm(h_all, norm_final_g)
    y_prompt = y[:N_PROMPT].reshape(BATCH, SEQ, D_MODEL)
    y_sample = y[N_PROMPT:].reshape(DEC_SEQ, DEC_BATCH, D_MODEL).transpose(1, 0, 2)
    return (y_prompt, y_sample,
            jnp.stack(kp_l), jnp.stack(vp_l), jnp.stack(cp_l),
            jnp.stack(ks_l), jnp.stack(vs_l), jnp.stack(cs_l))
```

```python
import itertools
import math

import jax
import jax.numpy as jnp
import numpy as np
from jax import lax
from jax.experimental import pallas as pl
from jax.experimental.pallas import tpu as pltpu

F32 = jnp.float32
BF16 = jnp.bfloat16

D_MODEL = 1024
BATCH = 4
SEQ = 4096
DEPTH = 4
DEC_BATCH = 128
DEC_SEQ = 4
HEAD_DIM = 64
N_HEADS = 16
N_KV_HEADS = 4
GROUP = N_HEADS // N_KV_HEADS
QKV_DIM = (N_HEADS + 2 * N_KV_HEADS) * HEAD_DIM
KV_DIM = N_KV_HEADS * HEAD_DIM
WINDOW = 128
BLOCK = 128
ATTN_SCALE = HEAD_DIM ** -0.5
NEG = -1e30
NUM_BUCKETS = 32
MAX_DISTANCE = 128
D_CONV = D_MODEL
CONV_W = 3
N_KEYS = 128
N_EXPERTS = N_KEYS * N_KEYS
PEER_HEADS = 8
D_KEY_HALF = 128
PEER_TOPK = 16
EPS = 1e-6

N_PROMPT = BATCH * SEQ
N_SAMPLE = DEC_BATCH * DEC_SEQ
N_TOK = N_PROMPT + N_SAMPLE

LANES = 128
SUBLANES = 8
PACKED_ROWS = 16
VMEM_LIMIT_BYTES = 56 * 1024 * 1024

ROW_TILE = 512
N_ROW_TILES = N_TOK // ROW_TILE
PROMPT_ROW_TILES = N_PROMPT // ROW_TILE
TILES_PER_SEQ = SEQ // ROW_TILE
ROUTE_TILE = 256
DENSE_TILE = 512
EXPERT_BLOCK = 2048
KEYS_PER_BLOCK = EXPERT_BLOCK // N_KEYS
KEYS_PER_GATE_GROUP = 2
CHUNKS_PER_GATE_GROUP = 4
SAMPLE_SEQ_BLOCK = 8
KEYS_SAMPLE = WINDOW + DEC_SEQ
KEYS_SAMPLE_PAD = 2 * LANES
CAND_MAIN_ROWS = 8
CAND_ROWS = CAND_MAIN_ROWS * PEER_TOPK + (PEER_TOPK - CAND_MAIN_ROWS)
NOT_SELECTED_RANK = 127.0


def _params(semantics):
    return pltpu.CompilerParams(dimension_semantics=semantics, vmem_limit_bytes=VMEM_LIMIT_BYTES)


def _rmsnorm(x, g):
    ms = jnp.mean(x * x, axis=-1, keepdims=True)
    return x * lax.rsqrt(ms + EPS) * g


def _norm_matmul_kernel(x_ref, g_ref, w_ref, o_ref):
    xn = _rmsnorm(x_ref[...], g_ref[...]).astype(BF16)
    o_ref[...] = jnp.dot(xn, w_ref[...], preferred_element_type=F32)


def _norm_matmul(x, g, w_bf16):
    n_out = w_bf16.shape[1]
    return pl.pallas_call(
        _norm_matmul_kernel,
        out_shape=jax.ShapeDtypeStruct((N_TOK, n_out), F32),
        grid=(N_ROW_TILES,),
        in_specs=[
            pl.BlockSpec((ROW_TILE, D_MODEL), lambda i: (i, 0)),
            pl.BlockSpec((1, D_MODEL), lambda i: (0, 0)),
            pl.BlockSpec((D_MODEL, n_out), lambda i: (0, 0)),
        ],
        out_specs=pl.BlockSpec((ROW_TILE, n_out), lambda i: (i, 0)),
        compiler_params=_params(("parallel",)),
        name="norm_matmul",
    )(x, g.reshape(1, D_MODEL), w_bf16)


def _matmul_residual_kernel(a_ref, w_ref, r_ref, o_ref):
    o_ref[...] = r_ref[...] + jnp.dot(a_ref[...], w_ref[...], preferred_element_type=F32)


def _matmul_residual(a_bf16, w_bf16, res):
    return pl.pallas_call(
        _matmul_residual_kernel,
        out_shape=jax.ShapeDtypeStruct((N_TOK, D_MODEL), F32),
        grid=(N_ROW_TILES,),
        in_specs=[
            pl.BlockSpec((ROW_TILE, D_MODEL), lambda i: (i, 0)),
            pl.BlockSpec((D_MODEL, D_MODEL), lambda i: (0, 0)),
            pl.BlockSpec((ROW_TILE, D_MODEL), lambda i: (i, 0)),
        ],
        out_specs=pl.BlockSpec((ROW_TILE, D_MODEL), lambda i: (i, 0)),
        compiler_params=_params(("parallel",)),
        name="matmul_residual",
    )(a_bf16, w_bf16, res)


def _final_norm_kernel(x_ref, g_ref, o_ref):
    o_ref[...] = _rmsnorm(x_ref[...], g_ref[...])


def _final_norm(x, g):
    return pl.pallas_call(
        _final_norm_kernel,
        out_shape=jax.ShapeDtypeStruct((N_TOK, D_MODEL), F32),
        grid=(N_ROW_TILES,),
        in_specs=[
            pl.BlockSpec((ROW_TILE, D_MODEL), lambda i: (i, 0)),
            pl.BlockSpec((1, D_MODEL), lambda i: (0, 0)),
        ],
        out_specs=pl.BlockSpec((ROW_TILE, D_MODEL), lambda i: (i, 0)),
        compiler_params=_params(("parallel",)),
        name="final_norm",
    )(x, g.reshape(1, D_MODEL))


def _t5_bucket_np(dist):
    n = np.maximum(dist, 0)
    max_exact = NUM_BUCKETS // 2
    nf = np.maximum(n, 1).astype(np.float32)
    large = max_exact + (np.log(nf / np.float32(max_exact)) / np.float32(math.log(MAX_DISTANCE / max_exact))
                         * np.float32(NUM_BUCKETS - max_exact)).astype(np.int32)
    large = np.minimum(large, NUM_BUCKETS - 1)
    return np.where(n < max_exact, n, large).astype(np.int32)


def _bias_table_kernel(rel_ref, bucket_ref, o_ref):
    bucket = bucket_ref[...]
    for h in range(N_HEADS):
        acc = jnp.full(bucket.shape, NEG, F32)
        for b in range(NUM_BUCKETS):
            acc = jnp.where(bucket == b, rel_ref[b, h], acc)
        o_ref[h] = acc


def _bias_table(rel_bias, bucket_np):
    rows, cols = bucket_np.shape
    return pl.pallas_call(
        _bias_table_kernel,
        out_shape=jax.ShapeDtypeStruct((N_HEADS, rows, cols), F32),
        in_specs=[
            pl.BlockSpec(memory_space=pltpu.SMEM),
            pl.BlockSpec((rows, cols), lambda: (0, 0)),
        ],
        out_specs=pl.BlockSpec((N_HEADS, rows, cols), lambda: (0, 0, 0)),
        name="bias_table",
    )(rel_bias, jnp.asarray(bucket_np))


def _prompt_buckets():
    r = np.arange(BLOCK)[:, None]
    c = np.arange(2 * BLOCK)[None, :]
    dist = r - c + BLOCK
    valid = (dist >= 0) & (dist < WINDOW)
    rest = np.where(valid, _t5_bucket_np(dist), -1).astype(np.int32)
    first = np.where(c >= BLOCK, rest, -1).astype(np.int32)
    return np.concatenate([first, rest], axis=0)


def _sample_buckets():
    j = np.arange(DEC_SEQ)[:, None]
    c = np.arange(KEYS_SAMPLE_PAD)[None, :]
    dist = j + WINDOW - c
    valid = (dist >= 0) & (dist < WINDOW) & (c < KEYS_SAMPLE)
    return np.where(valid, _t5_bucket_np(dist), -1).astype(np.int32)


def _softmax_sink_pv(s_list, v_list, sink):
    m = sink
    for s in s_list:
        m = jnp.maximum(m, jnp.max(s, axis=-1, keepdims=True))
    denom = jnp.exp(sink - m)
    acc = None
    for s, v in zip(s_list, v_list):
        p = jnp.exp(s - m)
        denom = denom + jnp.sum(p, axis=-1, keepdims=True)
        pv = jnp.dot(p.astype(BF16), v, preferred_element_type=F32)
        acc = pv if acc is None else acc + pv
    return acc / denom


def _attn_prompt_kernel(sink_ref, q_ref, kp_ref, kc_ref, vp_ref, vc_ref, bias_ref, o_ref):
    kp = kp_ref[...].astype(BF16)
    kc = kc_ref[...].astype(BF16)
    vp = vp_ref[...].astype(BF16)
    vc = vc_ref[...].astype(BF16)
    outs = []
    for g in range(N_KV_HEADS):
        ksl = slice(g * HEAD_DIM, (g + 1) * HEAD_DIM)
        qg = jnp.concatenate(
            [q_ref[:, (g * GROUP + hh) * HEAD_DIM:(g * GROUP + hh + 1) * HEAD_DIM] for hh in range(GROUP)],
            axis=0).astype(BF16)
        bias = jnp.concatenate([bias_ref[g * GROUP + hh] for hh in range(GROUP)], axis=0)
        sink = jnp.concatenate(
            [jnp.full((BLOCK, 1), sink_ref[g * GROUP + hh], F32) for hh in range(GROUP)], axis=0)
        dn = (((1,), (1,)), ((), ()))
        s_prev = lax.dot_general(qg, kp[:, ksl], dn, preferred_element_type=F32) * ATTN_SCALE + bias[:, :BLOCK]
        s_cur = lax.dot_general(qg, kc[:, ksl], dn, preferred_element_type=F32) * ATTN_SCALE + bias[:, BLOCK:]
        og = _softmax_sink_pv([s_prev, s_cur], [vp[:, ksl], vc[:, ksl]], sink)
        for hh in range(GROUP):
            outs.append(og[hh * BLOCK:(hh + 1) * BLOCK])
    o_ref[...] = jnp.concatenate(outs, axis=1).astype(o_ref.dtype)


def _attn_prompt(qkv, sinks, bias_tab):
    nb = SEQ // BLOCK
    kcol = N_HEADS * HEAD_DIM // KV_DIM
    row = lambda b, n: b * nb + n
    prev = lambda b, n: b * nb + jnp.maximum(n - 1, 0)
    return pl.pallas_call(
        _attn_prompt_kernel,
        out_shape=jax.ShapeDtypeStruct((N_PROMPT, D_MODEL), BF16),
        grid=(BATCH, nb),
        in_specs=[
            pl.BlockSpec(memory_space=pltpu.SMEM),
            pl.BlockSpec((BLOCK, N_HEADS * HEAD_DIM), lambda b, n: (row(b, n), 0)),
            pl.BlockSpec((BLOCK, KV_DIM), lambda b, n: (prev(b, n), kcol)),
            pl.BlockSpec((BLOCK, KV_DIM), lambda b, n: (row(b, n), kcol)),
            pl.BlockSpec((BLOCK, KV_DIM), lambda b, n: (prev(b, n), kcol + 1)),
            pl.BlockSpec((BLOCK, KV_DIM), lambda b, n: (row(b, n), kcol + 1)),
            pl.BlockSpec((N_HEADS, BLOCK, 2 * BLOCK), lambda b, n: (0, jnp.minimum(n, 1), 0)),
        ],
        out_specs=pl.BlockSpec((BLOCK, D_MODEL), lambda b, n: (row(b, n), 0)),
        compiler_params=_params(("parallel", "parallel")),
        name="attn_prompt",
    )(sinks, qkv, qkv, qkv, qkv, qkv, bias_tab)


def _attn_sample_kernel(q_ref, k_ref, v_ref, bias_ref, sink_ref, o_ref):
    nb = SAMPLE_SEQ_BLOCK * N_KV_HEADS
    rows = DEC_SEQ * GROUP
    q = q_ref[...].reshape(nb, rows, HEAD_DIM)
    k = k_ref[...].reshape(nb, KEYS_SAMPLE_PAD, HEAD_DIM)
    v = v_ref[...].reshape(nb, KEYS_SAMPLE_PAD, HEAD_DIM)
    s = jnp.einsum("nqd,nkd->nqk", q, k, preferred_element_type=F32) * ATTN_SCALE
    s = s.reshape(SAMPLE_SEQ_BLOCK, N_KV_HEADS, rows, KEYS_SAMPLE_PAD) + bias_ref[...][None]
    sink = sink_ref[...][None]
    m = jnp.maximum(jnp.max(s, axis=-1, keepdims=True), sink)
    p = jnp.exp(s - m)
    denom = jnp.sum(p, axis=-1, keepdims=True) + jnp.exp(sink - m)
    pv = jnp.einsum("nqk,nkd->nqd", p.reshape(nb, rows, KEYS_SAMPLE_PAD).astype(BF16), v,
                    preferred_element_type=F32)
    o_ref[...] = pv.reshape(SAMPLE_SEQ_BLOCK, N_KV_HEADS, rows, HEAD_DIM) / denom


def _attn_sample(qg, kc, vc, bias_s, sink_s):
    rows = DEC_SEQ * GROUP
    blk = lambda shape: pl.BlockSpec((SAMPLE_SEQ_BLOCK,) + shape, lambda i: (i, 0, 0, 0))
    return pl.pallas_call(
        _attn_sample_kernel,
        out_shape=jax.ShapeDtypeStruct((DEC_BATCH, N_KV_HEADS, rows, HEAD_DIM), F32),
        grid=(DEC_BATCH // SAMPLE_SEQ_BLOCK,),
        in_specs=[
            blk((N_KV_HEADS, rows, HEAD_DIM)),
            blk((N_KV_HEADS, KEYS_SAMPLE_PAD, HEAD_DIM)),
            blk((N_KV_HEADS, KEYS_SAMPLE_PAD, HEAD_DIM)),
            pl.BlockSpec((N_KV_HEADS, rows, KEYS_SAMPLE_PAD), lambda i: (0, 0, 0)),
            pl.BlockSpec((N_KV_HEADS, rows, 1), lambda i: (0, 0, 0)),
        ],
        out_specs=blk((N_KV_HEADS, rows, HEAD_DIM)),
        compiler_params=_params(("parallel",)),
        name="attn_sample",
    )(qg, kc, vc, bias_s, sink_s)


def _conv_prompt_kernel(bch_ref, w_ref, wout_ref, r_ref, o_ref, ulast_ref, ubuf):
    i = pl.program_id(0)

    @pl.when(i % TILES_PER_SEQ == 0)
    def _():
        ubuf[0:SUBLANES, :] = jnp.zeros((SUBLANES, D_CONV), F32)

    u = bch_ref[:, D_CONV:2 * D_CONV] * bch_ref[:, 2 * D_CONV:]
    ubuf[SUBLANES:, :] = u
    y = (w_ref[0:1, :] * ubuf[SUBLANES - 2:SUBLANES - 2 + ROW_TILE, :]
         + w_ref[1:2, :] * ubuf[SUBLANES - 1:SUBLANES - 1 + ROW_TILE, :]
         + w_ref[2:3, :] * u)
    gated = (bch_ref[:, :D_CONV] * y).astype(BF16)
    o_ref[...] = r_ref[...] + jnp.dot(gated, wout_ref[...], preferred_element_type=F32)
    tail = u[ROW_TILE - SUBLANES:, :]
    ulast_ref[...] = tail
    ubuf[0:SUBLANES, :] = tail


def _conv_prompt(bch, conv_w, wout_bf16, res):
    return pl.pallas_call(
        _conv_prompt_kernel,
        out_shape=(jax.ShapeDtypeStruct((N_PROMPT, D_MODEL), F32),
                   jax.ShapeDtypeStruct((PROMPT_ROW_TILES * SUBLANES, D_CONV), F32)),
        grid=(PROMPT_ROW_TILES,),
        in_specs=[
            pl.BlockSpec((ROW_TILE, 3 * D_CONV), lambda i: (i, 0)),
            pl.BlockSpec((CONV_W, D_CONV), lambda i: (0, 0)),
            pl.BlockSpec((D_CONV, D_MODEL), lambda i: (0, 0)),
            pl.BlockSpec((ROW_TILE, D_MODEL), lambda i: (i, 0)),
        ],
        out_specs=(pl.BlockSpec((ROW_TILE, D_MODEL), lambda i: (i, 0)),
                   pl.BlockSpec((SUBLANES, D_CONV), lambda i: (i, 0))),
        scratch_shapes=[pltpu.VMEM((ROW_TILE + SUBLANES, D_CONV), F32)],
        compiler_params=_params(("arbitrary",)),
        name="conv_prompt",
    )(bch, conv_w, wout_bf16, res)


def _conv_sample_kernel(bch_ref, st_ref, w_ref, wout_ref, r_ref, o_ref, u_ref):
    u = bch_ref[:, D_CONV:2 * D_CONV] * bch_ref[:, 2 * D_CONV:]
    u_ref[...] = u
    up = [st_ref[0:DEC_BATCH, :], st_ref[DEC_BATCH:, :]] + [u[t * DEC_BATCH:(t + 1) * DEC_BATCH] for t in range(DEC_SEQ)]
    y = jnp.concatenate(
        [w_ref[0:1, :] * up[t] + w_ref[1:2, :] * up[t + 1] + w_ref[2:3, :] * up[t + 2] for t in range(DEC_SEQ)], axis=0)
    gated = (bch_ref[:, :D_CONV] * y).astype(BF16)
    o_ref[...] = r_ref[...] + jnp.dot(gated, wout_ref[...], preferred_element_type=F32)


def _conv_sample(bch_s, state_tm, conv_w, wout_bf16, res_s):
    full = lambda shape: pl.BlockSpec(shape, lambda: (0,) * len(shape))
    return pl.pallas_call(
        _conv_sample_kernel,
        out_shape=(jax.ShapeDtypeStruct((N_SAMPLE, D_MODEL), F32),
                   jax.ShapeDtypeStruct((N_SAMPLE, D_CONV), F32)),
        in_specs=[full((N_SAMPLE, 3 * D_CONV)), full(((CONV_W - 1) * DEC_BATCH, D_CONV)),
                  full((CONV_W, D_CONV)), full((D_CONV, D_MODEL)), full((N_SAMPLE, D_MODEL))],
        out_specs=(full((N_SAMPLE, D_MODEL)), full((N_SAMPLE, D_CONV))),
        compiler_params=pltpu.CompilerParams(vmem_limit_bytes=VMEM_LIMIT_BYTES),
        name="conv_sample",
    )(bch_s, state_tm, conv_w, wout_bf16, res_s)


def _extract_max(vals, idx):
    m = jnp.max(vals, axis=0, keepdims=True)
    first = jnp.min(jnp.where(vals == m, idx, jnp.float32(1e9)), axis=0, keepdims=True)
    hit = idx == first
    return m, hit, jnp.where(hit, -jnp.inf, vals)


def _bf16_high_bits(v):
    bits = pltpu.bitcast(v, jnp.uint32)
    return (bits + jnp.uint32(0x7FFF) + ((bits >> 16) & jnp.uint32(1))) & jnp.uint32(0xFFFF0000)


def _bf16_pair_words(v):
    high = _bf16_high_bits(v)
    return high | (high >> 16)


def _bf16_split_words(v):
    half = v.shape[0] // 2
    return _bf16_high_bits(v[half:]) | (_bf16_high_bits(v[:half]) >> 16)


def _peer_route_kernel(x_ref, g_ref, wq_ref, sk_ref, cidx_ref, cvalid_ref,
                       xnt_ref, nsel_ref, a_ref, rb_ref, b_ref,
                       q_scr, s_scr, rank_scr, sorted_scr):
    xn = _rmsnorm(x_ref[...], g_ref[...])
    xnt = xn.T.astype(BF16)
    xnt_ref[...] = xnt
    q_scr[...] = jnp.dot(wq_ref[...], xnt, preferred_element_type=F32).astype(BF16)
    key_idx = lax.broadcasted_iota(jnp.int32, (N_KEYS, LANES), 0).astype(F32)
    n_lane_tiles = ROUTE_TILE // LANES

    def stage1(hp, carry):
        qs = q_scr[pl.ds(pl.multiple_of(hp * D_KEY_HALF, D_KEY_HALF), D_KEY_HALF), :]
        s_all = jnp.dot(sk_ref[hp], qs, preferred_element_type=F32)
        s_scr[hp] = s_all
        for lt in range(n_lane_tiles):
            lanes = slice(lt * LANES, (lt + 1) * LANES)
            vals = s_all[:, lanes]
            rank = jnp.full((N_KEYS, LANES), NOT_SELECTED_RANK, F32)
            for it in range(PEER_TOPK):
                m, hit, vals = _extract_max(vals, key_idx)
                rank = jnp.where(hit, jnp.float32(it), rank)
                sorted_scr[hp, it:it + 1, lanes] = m
            rank_scr[hp, :, lanes] = rank
        return carry

    lax.fori_loop(0, 2 * PEER_HEADS, stage1, 0)

    cidx = cidx_ref[...]
    cvalid = cvalid_ref[...] > 0.5

    def stage2(h, carry):
        for lt in range(n_lane_tiles):
            lanes = slice(lt * LANES, (lt + 1) * LANES)
            a_s = sorted_scr[2 * h, :, lanes]
            b_s = sorted_scr[2 * h + 1, :, lanes]
            main = [a_s[r:r + 1, :] + b_s for r in range(CAND_MAIN_ROWS)]
            tail = a_s[CAND_MAIN_ROWS:, :] + b_s[0:1, :]
            cand = jnp.where(cvalid, jnp.concatenate(main + [tail], axis=0), -jnp.inf)
            sel = jnp.zeros((CAND_ROWS, LANES), F32)
            top = a_s[0:1, :] + b_s[0:1, :]
            z = jnp.zeros((1, LANES), F32)
            for it in range(PEER_TOPK):
                m, hit, cand = _extract_max(cand, cidx)
                sel = jnp.where(hit, 1.0, sel)
                z = z + jnp.exp(m - top)
            rank_a = rank_scr[2 * h, :, lanes]
            nsel = jnp.zeros((N_KEYS, LANES), F32)
            for r in range(PEER_TOPK):
                if r < CAND_MAIN_ROWS:
                    n_r = jnp.sum(sel[r * PEER_TOPK:(r + 1) * PEER_TOPK, :], axis=0, keepdims=True)
                else:
                    row = CAND_MAIN_ROWS * PEER_TOPK + r - CAND_MAIN_ROWS
                    n_r = sel[row:row + 1, :]
                nsel = jnp.where(rank_a == jnp.float32(r), n_r, nsel)
            nsel_ref[h, :, lanes] = _bf16_pair_words(nsel)
            a_ref[h, :, lanes] = _bf16_pair_words(jnp.exp(s_scr[2 * h, :, lanes] - a_s[0:1, :]))
            rb_ref[h, :, lanes] = _bf16_split_words(rank_scr[2 * h + 1, :, lanes])
            b_ref[h, :, lanes] = _bf16_split_words(jnp.exp(s_scr[2 * h + 1, :, lanes] - b_s[0:1, :]) / z)
        return carry

    lax.fori_loop(0, PEER_HEADS, stage2, 0)


def _candidate_tables():
    rows = np.arange(CAND_ROWS)
    main = rows < CAND_MAIN_ROWS * PEER_TOPK
    r = np.where(main, rows // PEER_TOPK, CAND_MAIN_ROWS + rows - CAND_MAIN_ROWS * PEER_TOPK)
    c = np.where(main, rows % PEER_TOPK, 0)
    idx = (r * PEER_TOPK + c).astype(np.float32)
    valid = ((r + 1) * (c + 1) <= PEER_TOPK).astype(np.float32)
    tile = lambda v: np.ascontiguousarray(np.broadcast_to(v[:, None], (CAND_ROWS, LANES)))
    return tile(idx), tile(valid)


def _peer_route(h_all, g, wq_t_bf16, sub_keys_bf16):
    cidx, cvalid = _candidate_tables()
    n_tiles = N_TOK // ROUTE_TILE
    words = jax.ShapeDtypeStruct((PEER_HEADS, N_KEYS, N_TOK), jnp.uint32)
    halves = jax.ShapeDtypeStruct((PEER_HEADS, N_KEYS // 2, N_TOK), jnp.uint32)
    head_spec = pl.BlockSpec((PEER_HEADS, N_KEYS, ROUTE_TILE), lambda i: (0, 0, i))
    half_spec = pl.BlockSpec((PEER_HEADS, N_KEYS // 2, ROUTE_TILE), lambda i: (0, 0, i))
    return pl.pallas_call(
        _peer_route_kernel,
        out_shape=(jax.ShapeDtypeStruct((D_MODEL, N_TOK), BF16), words, words, halves, halves),
        grid=(n_tiles,),
        in_specs=[
            pl.BlockSpec((ROUTE_TILE, D_MODEL), lambda i: (i, 0)),
            pl.BlockSpec((1, D_MODEL), lambda i: (0, 0)),
            pl.BlockSpec((2 * PEER_HEADS * D_KEY_HALF, D_MODEL), lambda i: (0, 0)),
            pl.BlockSpec((2 * PEER_HEADS, N_KEYS, D_KEY_HALF), lambda i: (0, 0, 0)),
            pl.BlockSpec((CAND_ROWS, LANES), lambda i: (0, 0)),
            pl.BlockSpec((CAND_ROWS, LANES), lambda i: (0, 0)),
        ],
        out_specs=(pl.BlockSpec((D_MODEL, ROUTE_TILE), lambda i: (0, i)),
                   head_spec, head_spec, half_spec, half_spec),
        scratch_shapes=[
            pltpu.VMEM((2 * PEER_HEADS * D_KEY_HALF, ROUTE_TILE), BF16),
            pltpu.VMEM((2 * PEER_HEADS, N_KEYS, ROUTE_TILE), F32),
            pltpu.VMEM((2 * PEER_HEADS, N_KEYS, ROUTE_TILE), F32),
            pltpu.VMEM((2 * PEER_HEADS, PEER_TOPK, ROUTE_TILE), F32),
        ],
        compiler_params=_params(("parallel",)),
        name="peer_route",
    )(h_all, g.reshape(1, D_MODEL), wq_t_bf16, sub_keys_bf16, jnp.asarray(cidx), jnp.asarray(cvalid))


def _gelu(x):
    return 0.5 * x * (1.0 + lax.erf(x * np.float32(1.0 / math.sqrt(2.0))))


def _packed_row_broadcast(words, row):
    return pltpu.bitcast(jnp.broadcast_to(words[row:row + 1, :], (SUBLANES, LANES)), BF16)


def _peer_dense_kernel(xnt_ref, u_ref, vt_ref, nsel_ref, a_ref, rb_ref, b_ref, res_ref, o_ref,
                       acc_scr, h_scr, w_scr, z_new_scr, z_prev_scr):
    e = pl.program_id(1)
    n_blocks = pl.num_programs(1) - 1

    @pl.when(e == 0)
    def _():
        acc_scr[...] = jnp.zeros_like(acc_scr)
        z_new_scr[...] = jnp.zeros_like(z_new_scr)

    @pl.when(e < n_blocks)
    def _():
        z_prev_scr[...] = z_new_scr[...]
        first_keys = pl.ds(pl.multiple_of(e * KEYS_PER_BLOCK, SUBLANES), KEYS_PER_BLOCK)
        n_chunks = N_KEYS // PACKED_ROWS
        zero = jnp.zeros((PACKED_ROWS, LANES), BF16)
        for i0 in range(0, KEYS_PER_BLOCK, KEYS_PER_GATE_GROUP):
            group = range(i0, i0 + KEYS_PER_GATE_GROUP)
            for lt, k0 in itertools.product(range(DENSE_TILE // LANES), range(0, n_chunks, CHUNKS_PER_GATE_GROUP)):
                lanes = slice(lt * LANES, (lt + 1) * LANES)
                chunks = range(k0, k0 + CHUNKS_PER_GATE_GROUP)
                w = {(ii, k): zero for ii in group for k in chunks}
                for h in range(PEER_HEADS):
                    n_words = nsel_ref[h, first_keys, lanes]
                    a_words = a_ref[h, first_keys, lanes]
                    n_b = {ii: _packed_row_broadcast(n_words, ii) for ii in group}
                    a_b = {ii: _packed_row_broadcast(a_words, ii) for ii in group}
                    for k in chunks:
                        second = slice(k * SUBLANES, (k + 1) * SUBLANES)
                        rank_b = pltpu.bitcast(rb_ref[h, second, lanes], BF16)
                        gate_b = pltpu.bitcast(b_ref[h, second, lanes], BF16)
                        for ii in group:
                            w[ii, k] = w[ii, k] + a_b[ii] * jnp.where(rank_b < n_b[ii], gate_b, zero)
                for ii in group:
                    for k in chunks:
                        words = slice((ii * n_chunks + k) * SUBLANES, (ii * n_chunks + k + 1) * SUBLANES)
                        w_scr[words, lanes] = pltpu.bitcast(w[ii, k], jnp.uint32)
        h_scr[...] = jnp.dot(u_ref[...], xnt_ref[...], preferred_element_type=F32)
        acc_scr[...] += jnp.dot(vt_ref[...], z_prev_scr[...], preferred_element_type=F32)
        for ii, m, lt in itertools.product(range(KEYS_PER_BLOCK), range(n_chunks // 2), range(DENSE_TILE // LANES)):
            lanes = slice(lt * LANES, (lt + 1) * LANES)
            first_word = (ii * n_chunks + 2 * m) * SUBLANES
            words = w_scr[first_word:first_word + PACKED_ROWS, lanes]
            gates = (pltpu.bitcast(words << 16, F32), pltpu.bitcast(words & jnp.uint32(0xFFFF0000), F32))
            for half, gate in enumerate(gates):
                row0 = ii * N_KEYS + half * (N_KEYS // 2) + m * PACKED_ROWS
                rows = slice(row0, row0 + PACKED_ROWS)
                z_new_scr[rows, lanes] = (_gelu(h_scr[rows, lanes]) * gate).astype(BF16)

    @pl.when(e == n_blocks)
    def _():
        z_prev_scr[...] = z_new_scr[...]
        acc = acc_scr[...] + jnp.dot(vt_ref[...], z_prev_scr[...], preferred_element_type=F32)
        o_ref[...] = res_ref[...] + acc.T


def _peer_dense(xnt, u_bf16, vt_bf16, nsel, a, rb, b, res):
    n_blocks = N_EXPERTS // EXPERT_BLOCK
    head_spec = pl.BlockSpec((PEER_HEADS, N_KEYS, DENSE_TILE), lambda t, e: (0, 0, t))
    half_spec = pl.BlockSpec((PEER_HEADS, N_KEYS // 2, DENSE_TILE), lambda t, e: (0, 0, t))
    return pl.pallas_call(
        _peer_dense_kernel,
        out_shape=jax.ShapeDtypeStruct((N_TOK, D_MODEL), F32),
        grid=(N_TOK // DENSE_TILE, n_blocks + 1),
        in_specs=[
            pl.BlockSpec((D_MODEL, DENSE_TILE), lambda t, e: (0, t)),
            pl.BlockSpec((EXPERT_BLOCK, D_MODEL), lambda t, e: (jnp.minimum(e, n_blocks - 1), 0)),
            pl.BlockSpec((D_MODEL, EXPERT_BLOCK), lambda t, e: (0, jnp.maximum(e - 1, 0))),
            head_spec, head_spec, half_spec, half_spec,
            pl.BlockSpec((DENSE_TILE, D_MODEL), lambda t, e: (t, 0)),
        ],
        out_specs=pl.BlockSpec((DENSE_TILE, D_MODEL), lambda t, e: (t, 0)),
        scratch_shapes=[
            pltpu.VMEM((D_MODEL, DENSE_TILE), F32),
            pltpu.VMEM((EXPERT_BLOCK, DENSE_TILE), F32),
            pltpu.VMEM((EXPERT_BLOCK // 2, DENSE_TILE), jnp.uint32),
            pltpu.VMEM((EXPERT_BLOCK, DENSE_TILE), BF16),
            pltpu.VMEM((EXPERT_BLOCK, DENSE_TILE), BF16),
        ],
        compiler_params=pltpu.CompilerParams(
            dimension_semantics=("parallel", "arbitrary"), vmem_limit_bytes=VMEM_LIMIT_BYTES),
        name="peer_dense",
    )(xnt, u_bf16, vt_bf16, nsel, a, rb, b, res)


def _peer(h_all, g, w_q, sub_keys, u_tab, v_tab):
    wq_t = w_q.T.astype(BF16)
    sk = sub_keys.reshape(2 * PEER_HEADS, N_KEYS, D_KEY_HALF).astype(BF16)
    xnt, nsel, a, rb, b = _peer_route(h_all, g, wq_t, sk)
    return _peer_dense(xnt, u_tab.astype(BF16), v_tab.T.astype(BF16), nsel, a, rb, b, h_all)


def _attention_layer(h_all, g, w_qkv, sinks, w_o, cache_k, cache_v, bias_p, bias_s):
    qkv = _norm_matmul(h_all, g, w_qkv.astype(BF16))
    o_p = _attn_prompt(qkv, sinks, bias_p)

    hq = N_HEADS * HEAD_DIM
    qkv_s = qkv[N_PROMPT:].reshape(DEC_SEQ, DEC_BATCH, QKV_DIM)
    q_s = qkv_s[..., :hq].reshape(DEC_SEQ, DEC_BATCH, N_KV_HEADS, GROUP, HEAD_DIM)
    qg = q_s.transpose(1, 2, 0, 3, 4).reshape(DEC_BATCH, N_KV_HEADS, DEC_SEQ * GROUP, HEAD_DIM)
    k_new = qkv_s[..., hq:hq + KV_DIM].reshape(DEC_SEQ, DEC_BATCH, N_KV_HEADS, HEAD_DIM).transpose(1, 0, 2, 3)
    v_new = qkv_s[..., hq + KV_DIM:].reshape(DEC_SEQ, DEC_BATCH, N_KV_HEADS, HEAD_DIM).transpose(1, 0, 2, 3)
    kc = jnp.concatenate([cache_k, k_new], axis=1)
    vc = jnp.concatenate([cache_v, v_new], axis=1)
    pad = ((0, 0), (0, 0), (0, KEYS_SAMPLE_PAD - KEYS_SAMPLE), (0, 0))
    kc_g = jnp.pad(kc.transpose(0, 2, 1, 3), pad).astype(BF16)
    vc_g = jnp.pad(vc.transpose(0, 2, 1, 3), pad).astype(BF16)
    sink_s = jnp.broadcast_to(sinks.reshape(N_KV_HEADS, 1, GROUP), (N_KV_HEADS, DEC_SEQ, GROUP))
    sink_s = sink_s.reshape(N_KV_HEADS, DEC_SEQ * GROUP, 1)
    og = _attn_sample(qg.astype(BF16), kc_g, vc_g, bias_s, sink_s)
    o_s = og.reshape(DEC_BATCH, N_KV_HEADS, DEC_SEQ, GROUP, HEAD_DIM).transpose(2, 0, 1, 3, 4)
    o_s = o_s.reshape(N_SAMPLE, D_MODEL).astype(BF16)

    h_new = _matmul_residual(jnp.concatenate([o_p, o_s], axis=0), w_o.astype(BF16), h_all)

    k_p = qkv[:N_PROMPT, hq:hq + KV_DIM].reshape(BATCH, SEQ, N_KV_HEADS, HEAD_DIM)[:, SEQ - WINDOW:]
    v_p = qkv[:N_PROMPT, hq + KV_DIM:].reshape(BATCH, SEQ, N_KV_HEADS, HEAD_DIM)[:, SEQ - WINDOW:]
    return h_new, k_p, v_p, kc[:, -WINDOW:], vc[:, -WINDOW:]


def _conv_layer(h_all, g, w_in, conv_w, w_out, state):
    bch = _norm_matmul(h_all, g, w_in.astype(BF16))
    wout = w_out.astype(BF16)
    h_p, ulast = _conv_prompt(bch, conv_w, wout, h_all)
    state_tm = state.transpose(1, 0, 2).reshape((CONV_W - 1) * DEC_BATCH, D_CONV)
    h_s, u_s = _conv_sample(bch[N_PROMPT:], state_tm, conv_w, wout, h_all[N_PROMPT:])
    ulast = ulast.reshape(BATCH, TILES_PER_SEQ, SUBLANES, D_CONV)
    conv_p = ulast[:, -1, SUBLANES - (CONV_W - 1):, :]
    conv_s = u_s.reshape(DEC_SEQ, DEC_BATCH, D_CONV)[DEC_SEQ - (CONV_W - 1):].transpose(1, 0, 2)
    return jnp.concatenate([h_p, h_s], axis=0), conv_p, conv_s


def _bias_tables(rel_bias):
    bias_p = _bias_table(rel_bias, _prompt_buckets())
    sb = _bias_table(rel_bias, np.pad(_sample_buckets(), ((0, SUBLANES - DEC_SEQ), (0, 0)), constant_values=-1))
    sb = sb[:, :DEC_SEQ, :]
    bias_s = sb.reshape(N_KV_HEADS, GROUP, DEC_SEQ, KEYS_SAMPLE_PAD).transpose(0, 2, 1, 3)
    return bias_p, bias_s.reshape(N_KV_HEADS, DEC_SEQ * GROUP, KEYS_SAMPLE_PAD)


def kernel(x_prompt, x_sample, cache_k, cache_v, state_conv, norm_mix_g, norm_ffn_g, norm_final_g, rel_bias, attn_w_qkv, attn_sinks, attn_w_o, conv_w_in, conv_w, conv_w_out, peer_w_q, peer_sub_keys, peer_u, peer_v):
    h_all = jnp.concatenate(
        [x_prompt.reshape(N_PROMPT, D_MODEL), x_sample.transpose(1, 0, 2).reshape(N_SAMPLE, D_MODEL)], axis=0)
    bias_p, bias_s = _bias_tables(rel_bias)
    kp_l, vp_l, cp_l, ks_l, vs_l, cs_l = [], [], [], [], [], []
    for i in range(DEPTH):
        j = i // 2
        if i % 2 == 0:
            h_all, kp, vp, kn, vn = _attention_layer(
                h_all, norm_mix_g[i], attn_w_qkv[j], attn_sinks[j], attn_w_o[j], cache_k[j], cache_v[j],
                bias_p, bias_s)
            kp_l.append(kp); vp_l.append(vp); ks_l.append(kn); vs_l.append(vn)
        else:
            h_all, cp, cn = _conv_layer(h_all, norm_mix_g[i], conv_w_in[j], conv_w[j], conv_w_out[j], state_conv[j])
            cp_l.append(cp); cs_l.append(cn)
        h_all = _peer(h_all, norm_ffn_g[i], peer_w_q[i], peer_sub_keys[i], peer_u[i], peer_v[i])
    y = _final_norm(h_all, norm_final_g)
    y_prompt = y[:N_PROMPT].reshape(BATCH, SEQ, D_MODEL)
    y_sample = y[N_PROMPT:].reshape(DEC_SEQ, DEC_BATCH, D_MODEL).transpose(1, 0, 2)
    return (y_prompt, y_sample,
            jnp.stack(kp_l), jnp.stack(vp_l), jnp.stack(cp_l),
            jnp.stack(ks_l), jnp.stack(vs_l), jnp.stack(cs_l))
```
